```python
import math
import jax, jax.numpy as jnp
from jax import lax
import numpy as np

D_MODEL = 4096
BATCH = 2
SEQ = 4096
DEPTH = 2

N_A_LAYERS = DEPTH // 2
N_B_LAYERS = DEPTH - N_A_LAYERS

GLA_HEADS = 8
GLA_DK = D_MODEL // 2 // GLA_HEADS
GLA_DV = D_MODEL // GLA_HEADS
GLA_RANK = 16
GLA_TAU = 16.0
GLA_CHUNK = 64
GLA_IN = 2 * GLA_HEADS * GLA_DK + 2 * GLA_HEADS * GLA_DV

DIL_HEADS = 32
DIL_HD = D_MODEL // DIL_HEADS
DIL_PATTERNS = ((128, 1), (512, 4), (2048, 16))
N_BRANCH = len(DIL_PATTERNS)

D_FF = 4 * D_MODEL
LN_EPS = 1e-5
DEEPNORM_ALPHA = (2 * DEPTH) ** 0.25
DEEPNORM_BETA = (8 * DEPTH) ** -0.25

kernel_name = "yoco_gla_dilated_swa_hybrid"


def layer_norm(x, g, b):
    xf = x.astype(jnp.float32)
    mu = jnp.mean(xf, axis=-1, keepdims=True)
    var = jnp.mean(jnp.square(xf - mu), axis=-1, keepdims=True)
    y = (xf - mu) * lax.rsqrt(var + LN_EPS)
    return (y * g + b).astype(x.dtype)


def squared_relu_mlp(x, w1, w2):
    h = jnp.square(jax.nn.relu(x @ w1))
    return h @ w2


def gla_mixer(x, w_in, w_g1, w_g2, b_g, gn_g, gn_b, w_out):
    bsz, s_len, _ = x.shape
    H, dk, dv, C = GLA_HEADS, GLA_DK, GLA_DV, GLA_CHUNK
    n_chunks = s_len // C
    proj = x @ w_in
    q, k, v, r = jnp.split(proj, [H * dk, 2 * H * dk, 2 * H * dk + H * dv], axis=-1)
    g = jax.nn.log_sigmoid(((x @ w_g1) @ w_g2 + b_g).astype(jnp.float32)) / GLA_TAU

    def to_chunks(t, d):
        return t.reshape(bsz, n_chunks, C, H, d).transpose(0, 3, 1, 2, 4).astype(jnp.float32)

    qc = to_chunks(q, dk) * (dk ** -0.5)
    kc = to_chunks(k, dk)
    vc = to_chunks(v, dv)
    bcum = jnp.cumsum(to_chunks(g, dk), axis=3)
    b_ref = bcum[:, :, :, C // 2 - 1:C // 2, :]
    b_last = bcum[:, :, :, -1:, :]

    attn = jnp.einsum('bhnid,bhnjd->bhnij', qc * jnp.exp(bcum - b_ref), kc * jnp.exp(b_ref - bcum))
    causal = jnp.tril(jnp.ones((C, C), dtype=bool))
    attn = jnp.where(causal, attn, 0.0)
    o_intra = jnp.einsum('bhnij,bhnje->bhnie', attn, vc)

    q_s = qc * jnp.exp(bcum)
    k_s = kc * jnp.exp(b_last - bcum)
    decay = jnp.exp(b_last[:, :, :, 0, :])

    def step(state, inp):
        q_n, k_n, v_n, dec_n = inp
        o_n = jnp.einsum('bhid,bhde->bhie', q_n, state)
        state = dec_n[..., None] * state + jnp.einsum('bhjd,bhje->bhde', k_n, v_n)
        return state, o_n

    xs = (jnp.moveaxis(q_s, 2, 0), jnp.moveaxis(k_s, 2, 0), jnp.moveaxis(vc, 2, 0), jnp.moveaxis(decay, 2, 0))
    state0 = jnp.zeros((bsz, H, dk, dv), jnp.float32)
    _, o_inter = lax.scan(step, state0, xs)
    o = o_intra + jnp.moveaxis(o_inter, 0, 2)

    mu = jnp.mean(o, axis=-1, keepdims=True)
    var = jnp.mean(jnp.square(o - mu), axis=-1, keepdims=True)
    o = (o - mu) * lax.rsqrt(var + LN_EPS)
    o = o.transpose(0, 2, 3, 1, 4).reshape(bsz, s_len, H * dv) * gn_g + gn_b
    y = (jax.nn.silu(r.astype(jnp.float32)) * o).astype(x.dtype)
    return y @ w_out


def dilated_branch(q, k, v, window, dilation):
    bsz, s_len, H, hd = q.shape
    blk = window // dilation
    span = blk * dilation
    padded = -(-s_len // span) * span
    sub_len = padded // dilation
    n_blk = sub_len // blk

    def split(t):
        t = jnp.pad(t, ((0, 0), (0, padded - s_len), (0, 0), (0, 0)))
        t = t.reshape(bsz, sub_len, dilation, H, hd).transpose(0, 2, 3, 1, 4)
        return t.reshape(bsz, dilation, H, n_blk, blk, hd)

    def with_prev(t):
        prev = jnp.pad(t, ((0, 0), (0, 0), (0, 0), (1, 0), (0, 0), (0, 0)))[:, :, :, :-1]
        return jnp.concatenate([prev, t], axis=4)

    qb = split(q)
    kw = with_prev(split(k))
    vw = with_prev(split(v)).astype(jnp.float32)
    s = jnp.einsum('bdhnic,bdhnjc->bdhnij', qb, kw).astype(jnp.float32) * (hd ** -0.5)
    i_idx = jnp.arange(blk)[:, None]
    j_idx = jnp.arange(2 * blk)[None, :]
    diff = blk + i_idx - j_idx
    band = (diff >= 0) & (diff <= blk)
    not_before_start = (jnp.arange(n_blk)[:, None, None] > 0) | (j_idx >= blk)[None]
    mask = band[None] & not_before_start
    s = jnp.where(mask, s, -jnp.inf)
    m = jnp.max(s, axis=-1, keepdims=True)
    p = jnp.exp(s - m)
    l = jnp.sum(p, axis=-1, keepdims=True)
    o = jnp.einsum('bdhnij,bdhnjc->bdhnic', p, vw) / l
    lse = (m + jnp.log(l))[..., 0]

    def merge(t):
        tail = t.shape[5:]
        t = t.reshape((bsz, dilation, H, sub_len) + tail)
        t = jnp.moveaxis(t, 3, 1)
        return t.reshape((bsz, padded, H) + tail)[:, :s_len]

    return merge(o), merge(lse)


def dilated_mixer(x, w_q, kv, w_out):
    bsz, s_len, _ = x.shape
    q = (x @ w_q).reshape(bsz, s_len, N_BRANCH, DIL_HEADS, DIL_HD)
    outs, lses = [], []
    for gi, (window, dilation) in enumerate(DIL_PATTERNS):
        o_g, lse_g = dilated_branch(q[:, :, gi], kv[:, :, gi, 0], kv[:, :, gi, 1], window, dilation)
        outs.append(o_g)
        lses.append(lse_g)
    weights = jax.nn.softmax(jnp.stack(lses, axis=0), axis=0)
    o = jnp.sum(weights[..., None] * jnp.stack(outs, axis=0), axis=0)
    return o.reshape(bsz, s_len, DIL_HEADS * DIL_HD).astype(x.dtype) @ w_out


def setup_inputs(seed: int = 0) -> dict:
    key = jax.random.key(seed)
    ks = jax.random.split(key, 16)
    f32 = jnp.float32
    nrm = lambda k, shape, scale: jax.random.normal(k, shape, f32) * scale
    return {
        'x': nrm(ks[0], (BATCH, SEQ, D_MODEL), 1.0),
        'a_w_in': nrm(ks[1], (N_A_LAYERS, D_MODEL, GLA_IN), D_MODEL ** -0.5),
        'a_w_g1': nrm(ks[2], (N_A_LAYERS, D_MODEL, GLA_RANK), D_MODEL ** -0.5),
        'a_w_g2': nrm(ks[3], (N_A_LAYERS, GLA_RANK, GLA_HEADS * GLA_DK), GLA_RANK ** -0.5),
        'a_b_g': nrm(ks[4], (N_A_LAYERS, GLA_HEADS * GLA_DK), 0.1),
        'a_gn_g': 1.0 + nrm(ks[5], (N_A_LAYERS, GLA_HEADS * GLA_DV), 0.02),
        'a_gn_b': nrm(ks[6], (N_A_LAYERS, GLA_HEADS * GLA_DV), 0.02),
        'a_w_out': nrm(ks[7], (N_A_LAYERS, GLA_HEADS * GLA_DV, D_MODEL), (GLA_HEADS * GLA_DV) ** -0.5 * DEEPNORM_BETA),
        'b_w_q': nrm(ks[8], (N_B_LAYERS, D_MODEL, N_BRANCH * DIL_HEADS * DIL_HD), D_MODEL ** -0.5),
        'kv_w': nrm(ks[9], (D_MODEL, N_BRANCH * 2 * DIL_HEADS * DIL_HD), D_MODEL ** -0.5),
        'b_w_out': nrm(ks[10], (N_B_LAYERS, DIL_HEADS * DIL_HD, D_MODEL), (DIL_HEADS * DIL_HD) ** -0.5 * DEEPNORM_BETA),
        'mlp_w1': nrm(ks[11], (DEPTH, D_MODEL, D_FF), D_MODEL ** -0.5),
        'mlp_w2': nrm(ks[12], (DEPTH, D_FF, D_MODEL), D_FF ** -0.5 * DEEPNORM_BETA),
        'ln_g': 1.0 + nrm(ks[13], (DEPTH, 2, D_MODEL), 0.02),
        'ln_b': nrm(ks[14], (DEPTH, 2, D_MODEL), 0.02),
    }


def reference(x, a_w_in, a_w_g1, a_w_g2, a_b_g, a_gn_g, a_gn_b, a_w_out, b_w_q, kv_w, b_w_out, mlp_w1, mlp_w2, ln_g, ln_b):
    bsz, s_len, _ = x.shape
    h = x
    kv = None
    for layer in range(DEPTH):
        if layer < N_A_LAYERS:
            mix = gla_mixer(h, a_w_in[layer], a_w_g1[layer], a_w_g2[layer], a_b_g[layer],
                            a_gn_g[layer], a_gn_b[layer], a_w_out[layer])
        else:
            if layer == N_A_LAYERS:
                kv = (h @ kv_w).reshape(bsz, s_len, N_BRANCH, 2, DIL_HEADS, DIL_HD)
            lb = layer - N_A_LAYERS
            mix = dilated_mixer(h, b_w_q[lb], kv, b_w_out[lb])
        h = layer_norm(DEEPNORM_ALPHA * h + mix, ln_g[layer, 0], ln_b[layer, 0])
        h = layer_norm(DEEPNORM_ALPHA * h + squared_relu_mlp(h, mlp_w1[layer], mlp_w2[layer]),
                       ln_g[layer, 1], ln_b[layer, 1])
    return h
```

```python
import functools

import jax
import jax.numpy as jnp
from jax import lax
from jax.experimental import pallas as pl
from jax.experimental.pallas import tpu as pltpu

F32 = jnp.float32
BF16 = jnp.bfloat16

GLA_HEADS = 8
GLA_RANK = 16
GLA_TAU = 16.0
GLA_CHUNK = 64
DIL_HEADS = 32
DIL_HD = 128
LN_EPS = 1e-5
DEPTH = 2
DEEPNORM_ALPHA = (2 * DEPTH) ** 0.25

V7X_VMEM_BYTES = 64 * 1024 * 1024
VMEM_LIMIT = V7X_VMEM_BYTES - 8 * 1024 * 1024

PERM_RES = 16
PERM_RUN = 128
PERM_SPAN = PERM_RES * PERM_RUN


def _cparams(n_axes):
    return pltpu.CompilerParams(
        dimension_semantics=("arbitrary",) * n_axes,
        vmem_limit_bytes=VMEM_LIMIT,
    )


def _mm_kernel(x_ref, w_ref, o_ref, *, act):
    w = w_ref[...].astype(BF16)
    acc = jnp.dot(x_ref[...], w, preferred_element_type=F32)
    if act == "relu2":
        acc = jnp.square(jnp.maximum(acc, 0.0))
    o_ref[...] = acc.astype(o_ref.dtype)


def _matmul(x, w, layer, *, out_dtype, act=None, tm=1024, tn=512, name):
    m, k = x.shape
    n = w.shape[-1]
    assert m % tm == 0 and n % tn == 0 and w.shape[-2] == k
    if w.ndim == 3:
        w_spec = pl.BlockSpec((None, k, tn), lambda i, j: (layer, 0, j))
    else:
        w_spec = pl.BlockSpec((k, tn), lambda i, j: (0, j))
    return pl.pallas_call(
        functools.partial(_mm_kernel, act=act),
        grid=(m // tm, n // tn),
        in_specs=[pl.BlockSpec((tm, k), lambda i, j: (i, 0)), w_spec],
        out_specs=pl.BlockSpec((tm, tn), lambda i, j: (i, j)),
        out_shape=jax.ShapeDtypeStruct((m, n), out_dtype),
        compiler_params=_cparams(2),
        name=name,
    )(x, w)


def _mm_acc_kernel(x_ref, w_ref, o_ref):
    kk = pl.program_id(2)
    part = jnp.dot(x_ref[...], w_ref[...].astype(BF16), preferred_element_type=F32)

    @pl.when(kk == 0)
    def _():
        o_ref[...] = part

    @pl.when(kk > 0)
    def _():
        o_ref[...] += part


def _matmul_ktiled(x, w, layer, *, tm=1024, tn=2048, tk=512, name):
    m, k = x.shape
    n = w.shape[-1]
    assert m % tm == 0 and n % tn == 0 and k % tk == 0
    return pl.pallas_call(
        _mm_acc_kernel,
        grid=(n // tn, m // tm, k // tk),
        in_specs=[
            pl.BlockSpec((tm, tk), lambda j, i, kk: (i, kk)),
            pl.BlockSpec((None, tk, tn), lambda j, i, kk: (layer, kk, j)),
        ],
        out_specs=pl.BlockSpec((tm, tn), lambda j, i, kk: (i, j)),
        out_shape=jax.ShapeDtypeStruct((m, n), F32),
        compiler_params=_cparams(3),
        name=name,
    )(x, w)


def _gate_kernel(x_ref, w1_ref, w2_ref, b_ref, o_ref):
    t = jnp.dot(x_ref[...], w1_ref[...].astype(BF16), preferred_element_type=F32)
    z = jnp.dot(t.astype(BF16), w2_ref[...].astype(BF16), preferred_element_type=F32)
    z = z + b_ref[...]
    log_sig = jnp.minimum(z, 0.0) - jnp.log1p(jnp.exp(-jnp.abs(z)))
    o_ref[...] = log_sig / GLA_TAU


def _gate(x, w1, w2, b, *, tm=512):
    m, k = x.shape
    rank = w1.shape[-1]
    n = w2.shape[-1]
    return pl.pallas_call(
        _gate_kernel,
        grid=(m // tm,),
        in_specs=[
            pl.BlockSpec((tm, k), lambda i: (i, 0)),
            pl.BlockSpec((k, rank), lambda i: (0, 0)),
            pl.BlockSpec((rank, n), lambda i: (0, 0)),
            pl.BlockSpec((1, n), lambda i: (0, 0)),
        ],
        out_specs=pl.BlockSpec((tm, n), lambda i: (i, 0)),
        out_shape=jax.ShapeDtypeStruct((m, n), F32),
        compiler_params=_cparams(1),
        name="gla_gate",
    )(x, w1, w2, b)


def _cumsum_rows(tri_bf16, g):
    g_hi = g.astype(BF16)
    r1 = g - g_hi.astype(F32)
    g_mid = r1.astype(BF16)
    g_lo = (r1 - g_mid.astype(F32)).astype(BF16)
    out = jnp.dot(tri_bf16, g_hi, preferred_element_type=F32)
    out = out + jnp.dot(tri_bf16, g_mid, preferred_element_type=F32)
    out = out + jnp.dot(tri_bf16, g_lo, preferred_element_type=F32)
    return out


def _gla_kernel(q_ref, k_ref, v_ref, r_ref, g_ref, gng_ref, gnb_ref, y_ref,
                state_ref, o_scr, *, dk, chunks):
    @pl.when(pl.program_id(2) == 0)
    def _():
        state_ref[...] = jnp.zeros_like(state_ref)

    c_len = GLA_CHUNK
    row = lax.broadcasted_iota(jnp.int32, (c_len, c_len), 0)
    col = lax.broadcasted_iota(jnp.int32, (c_len, c_len), 1)
    causal = row >= col
    tri = causal.astype(BF16)
    nt = (((1,), (1,)), ((), ()))
    tn = (((0,), (0,)), ((), ()))

    for c in range(chunks):
        sl = slice(c * c_len, (c + 1) * c_len)
        bcum = _cumsum_rows(tri, g_ref[sl, :])
        b_mid = bcum[c_len // 2 - 1:c_len // 2, :]
        b_last = bcum[c_len - 1:c_len, :]
        qc = q_ref[sl, :] * (dk ** -0.5)
        kc = k_ref[sl, :]
        vc = v_ref[sl, :].astype(BF16)
        qa = (qc * jnp.exp(bcum - b_mid)).astype(BF16)
        ka = (kc * jnp.exp(b_mid - bcum)).astype(BF16)
        attn = lax.dot_general(qa, ka, nt, preferred_element_type=F32)
        attn = jnp.where(causal, attn, 0.0)
        o_intra = jnp.dot(attn.astype(BF16), vc, preferred_element_type=F32)
        qs = (qc * jnp.exp(bcum)).astype(BF16)
        ks = (kc * jnp.exp(b_last - bcum)).astype(BF16)
        st = state_ref[...]
        o_inter = lax.dot_general(qs, st.astype(BF16), nt, preferred_element_type=F32)
        upd = lax.dot_general(vc, ks, tn, preferred_element_type=F32)
        state_ref[...] = jnp.exp(b_last) * st + upd
        o_scr[sl, :] = o_intra + o_inter

    o = o_scr[...]
    mu = jnp.mean(o, axis=-1, keepdims=True)
    var = jnp.mean(jnp.square(o - mu), axis=-1, keepdims=True)
    o = (o - mu) * lax.rsqrt(var + LN_EPS)
    o = o * gng_ref[...] + gnb_ref[...]
    r = r_ref[...]
    silu = r * (1.0 / (1.0 + jnp.exp(-r)))
    y_ref[...] = (silu * o).astype(y_ref.dtype)


def _gla(proj, g, gn_g, gn_b, *, bsz, s_len, heads, tb=512):
    t_len = proj.shape[0]
    dk = g.shape[1] // heads
    dv = gn_g.shape[1] // heads
    assert proj.shape[1] == 2 * heads * dk + 2 * heads * dv and dv == 2 * dk
    assert s_len % tb == 0 and tb % GLA_CHUNK == 0
    nblk = s_len // tb
    rows = lambda b, h, i: b * nblk + i
    kernel = functools.partial(_gla_kernel, dk=dk, chunks=tb // GLA_CHUNK)
    return pl.pallas_call(
        kernel,
        grid=(bsz, heads, nblk),
        in_specs=[
            pl.BlockSpec((tb, dk), lambda b, h, i: (rows(b, h, i), h)),
            pl.BlockSpec((tb, dk), lambda b, h, i: (rows(b, h, i), heads + h)),
            pl.BlockSpec((tb, dv), lambda b, h, i: (rows(b, h, i), heads + h)),
            pl.BlockSpec((tb, dv), lambda b, h, i: (rows(b, h, i), 2 * heads + h)),
            pl.BlockSpec((tb, dk), lambda b, h, i: (rows(b, h, i), h)),
            pl.BlockSpec((1, dv), lambda b, h, i: (0, h)),
            pl.BlockSpec((1, dv), lambda b, h, i: (0, h)),
        ],
        out_specs=pl.BlockSpec((tb, dv), lambda b, h, i: (rows(b, h, i), h)),
        out_shape=jax.ShapeDtypeStruct((t_len, heads * dv), BF16),
        scratch_shapes=[pltpu.VMEM((dv, dk), F32), pltpu.VMEM((tb, dv), F32)],
        compiler_params=_cparams(3),
        name="gla_mixer",
    )(proj, proj, proj, proj, g, gn_g, gn_b)


def _ln_kernel(h_ref, mix_ref, g_ref, b_ref, of_ref, ob_ref):
    z = DEEPNORM_ALPHA * h_ref[...] + mix_ref[...]
    mu = jnp.mean(z, axis=-1, keepdims=True)
    var = jnp.mean(jnp.square(z - mu), axis=-1, keepdims=True)
    y = (z - mu) * lax.rsqrt(var + LN_EPS)
    y = y * g_ref[...] + b_ref[...]
    of_ref[...] = y
    ob_ref[...] = y.astype(BF16)


def _res_ln(h, mix, gain, bias, *, tm=256, name):
    m, d = h.shape
    row_spec = pl.BlockSpec((tm, d), lambda i: (i, 0))
    vec_spec = pl.BlockSpec((1, d), lambda i: (0, 0))
    return pl.pallas_call(
        _ln_kernel,
        grid=(m // tm,),
        in_specs=[row_spec, row_spec, vec_spec, vec_spec],
        out_specs=[row_spec, row_spec],
        out_shape=[jax.ShapeDtypeStruct((m, d), F32), jax.ShapeDtypeStruct((m, d), BF16)],
        compiler_params=_cparams(1),
        name=name,
    )(h, mix, gain.reshape(1, d), bias.reshape(1, d))


def _dilated_kernel(q1_ref, q4_ref, q16_ref, k1_ref, k4_ref, k16_ref,
                    v1_ref, v4_ref, v16_ref, o_ref, acc_ref, m_ref, l_ref, *, spans):
    blk = 128
    scale = DIL_HD ** -0.5
    nt = (((1,), (1,)), ((), ()))
    ridx = lax.broadcasted_iota(jnp.int32, (blk, blk), 0)
    cidx = lax.broadcasted_iota(jnp.int32, (blk, blk), 1)

    def gather(ref, starts, run):
        if len(starts) == 1:
            return ref[pl.ds(starts[0], run), :]
        return jnp.concatenate([ref[pl.ds(s, run), :] for s in starts], axis=0)

    def scatter(ref, starts, run, val):
        for n, s in enumerate(starts):
            ref[pl.ds(s, run), :] = val[n * run:(n + 1) * run, :]

    def branch(q_ref, k_ref, v_ref, n_blocks, starts_fn, run, pos_fn, first, last):
        pos_q = pos_fn(ridx)
        pos_k = pos_fn(cidx)
        mask_cur = pos_k <= pos_q

        def body(n, carry):
            cur, prev, has_prev = starts_fn(n)
            q = gather(q_ref, cur, run).astype(BF16)
            kk = jnp.concatenate([gather(k_ref, prev, run), gather(k_ref, cur, run)], axis=0)
            vv = jnp.concatenate([gather(v_ref, prev, run), gather(v_ref, cur, run)], axis=0)
            s = lax.dot_general(q, kk.astype(BF16), nt, preferred_element_type=F32) * scale
            mask_prev = pos_k >= pos_q + jnp.where(has_prev, 0, blk)
            mask = jnp.concatenate([mask_prev, mask_cur], axis=1)
            s = jnp.where(mask, s, -jnp.inf)
            m_blk = jnp.max(s, axis=-1, keepdims=True)
            if first:
                m_new = jnp.broadcast_to(m_blk, (blk, blk))
            else:
                m_old = gather(m_ref, cur, run)
                m_new = jnp.maximum(m_old, m_blk)
                alpha = jnp.exp(m_old - m_new)
            p = jnp.exp(s - jnp.concatenate([m_new, m_new], axis=1))
            l_new = jnp.broadcast_to(jnp.sum(p, axis=-1, keepdims=True), (blk, blk))
            acc = jnp.dot(p.astype(BF16), vv.astype(BF16), preferred_element_type=F32)
            if not first:
                l_new = alpha * gather(l_ref, cur, run) + l_new
                acc = alpha * gather(acc_ref, cur, run) + acc
            if last:
                scatter(o_ref, cur, run, (acc / l_new).astype(o_ref.dtype))
            else:
                scatter(m_ref, cur, run, m_new)
                scatter(l_ref, cur, run, l_new)
                scatter(acc_ref, cur, run, acc)
            return carry

        lax.fori_loop(0, n_blocks, body, 0)

    def al(x, mult):
        return pl.multiple_of(x, mult)

    def divmod_pow2(x, d):
        shift = d.bit_length() - 1
        assert d == 1 << shift
        return x >> shift, x & (d - 1)

    def starts_d1(n):
        def runs(b):
            span, sub = divmod_pow2(b, 16)
            return [al(span * PERM_SPAN + r * PERM_RUN + sub * 8, 8) for r in range(PERM_RES)]
        return runs(n), runs(jnp.maximum(n - 1, 0)), n > 0

    branch(q1_ref, k1_ref, v1_ref, spans * 16, starts_d1, 8,
           lambda i: 16 * (i & 7) + (i >> 3), True, False)

    def starts_d4(n):
        res, c = divmod_pow2(n, spans * 4)
        def runs(cc):
            span, sub = divmod_pow2(cc, 4)
            return [al(span * PERM_SPAN + (res + 4 * u) * PERM_RUN + sub * 32, 32) for u in range(4)]
        return runs(c), runs(jnp.maximum(c - 1, 0)), c > 0

    branch(q4_ref, k4_ref, v4_ref, spans * 16, starts_d4, 32,
           lambda i: 4 * (i & 31) + (i >> 5), False, False)

    def starts_d16(n):
        res, span = divmod_pow2(n, spans)
        def runs(sp):
            return [al(sp * PERM_SPAN + res * PERM_RUN, PERM_RUN)]
        return runs(span), runs(jnp.maximum(span - 1, 0)), span > 0

    branch(q16_ref, k16_ref, v16_ref, spans * 16, starts_d16, 128,
           lambda i: i, False, True)


def _dilated_attention(q, kv, *, bsz, s_len, heads):
    t_len = q.shape[0]
    hd = DIL_HD
    assert s_len % PERM_SPAN == 0
    spans = s_len // PERM_SPAN
    assert q.shape[1] == 3 * heads * hd and kv.shape[1] == 6 * heads * hd

    def qspec(g):
        return pl.BlockSpec((s_len, hd), lambda b, h: (b, g * heads + h))

    def kvspec(g, which):
        return pl.BlockSpec((s_len, hd), lambda b, h: (b, (2 * g + which) * heads + h))

    return pl.pallas_call(
        functools.partial(_dilated_kernel, spans=spans),
        grid=(bsz, heads),
        in_specs=[qspec(0), qspec(1), qspec(2),
                  kvspec(0, 0), kvspec(1, 0), kvspec(2, 0),
                  kvspec(0, 1), kvspec(1, 1), kvspec(2, 1)],
        out_specs=pl.BlockSpec((s_len, hd), lambda b, h: (b, h)),
        out_shape=jax.ShapeDtypeStruct((t_len, heads * hd), BF16),
        scratch_shapes=[pltpu.VMEM((s_len, hd), F32)] * 3,
        compiler_params=_cparams(2),
        name="dilated_attention",
    )(q, q, q, kv, kv, kv, kv, kv, kv)


def _to_perm(x, bsz, s_len):
    d = x.shape[-1]
    x = x.reshape(bsz, s_len // PERM_SPAN, PERM_RUN, PERM_RES, d)
    return x.transpose(0, 1, 3, 2, 4).reshape(bsz * s_len, d)


def _from_perm(x, bsz, s_len):
    d = x.shape[-1]
    x = x.reshape(bsz, s_len // PERM_SPAN, PERM_RES, PERM_RUN, d)
    return x.transpose(0, 1, 3, 2, 4).reshape(bsz, s_len, d)


def _mlp_block(h, hb, w1, w2, ln_g, ln_b, layer):
    hmid = _matmul(hb, w1, layer, out_dtype=BF16, act="relu2", name=f"mlp_up_{layer}")
    mix = _matmul_ktiled(hmid, w2, layer, name=f"mlp_down_{layer}")
    return _res_ln(h, mix, ln_g[layer, 1], ln_b[layer, 1], name=f"ln_mlp_{layer}")


def kernel(x, a_w_in, a_w_g1, a_w_g2, a_b_g, a_gn_g, a_gn_b, a_w_out, b_w_q, kv_w, b_w_out,
           mlp_w1, mlp_w2, ln_g, ln_b):
    bsz, s_len, d = x.shape
    t_len = bsz * s_len
    h = x.reshape(t_len, d)
    hb = h.astype(BF16)

    proj = _matmul(hb, a_w_in, 0, out_dtype=F32, name="gla_in_proj")
    g = _gate(hb, a_w_g1[0], a_w_g2[0], a_b_g[0].reshape(1, -1))
    y = _gla(proj, g, a_gn_g[0].reshape(1, -1), a_gn_b[0].reshape(1, -1),
             bsz=bsz, s_len=s_len, heads=GLA_HEADS)
    mix = _matmul(y, a_w_out, 0, out_dtype=F32, name="gla_out_proj")
    h, hb = _res_ln(h, mix, ln_g[0, 0], ln_b[0, 0], name="ln_mix_0")
    h, hb = _mlp_block(h, hb, mlp_w1, mlp_w2, ln_g, ln_b, 0)

    h = _to_perm(h, bsz, s_len)
    hb = h.astype(BF16)
    q = _matmul(hb, b_w_q, 0, out_dtype=F32, name="dil_q_proj")
    kv = _matmul(hb, kv_w, 0, out_dtype=F32, name="dil_kv_proj")
    o = _dilated_attention(q, kv, bsz=bsz, s_len=s_len, heads=DIL_HEADS)
    mix = _matmul(o, b_w_out, 0, out_dtype=F32, name="dil_out_proj")
    h, hb = _res_ln(h, mix, ln_g[1, 0], ln_b[1, 0], name="ln_mix_1")
    h, hb = _mlp_block(h, hb, mlp_w1, mlp_w2, ln_g, ln_b, 1)
    return _from_perm(h, bsz, s_len)
```

```python
import functools

import jax
import jax.numpy as jnp
from jax import lax
from jax.experimental import pallas as pl
from jax.experimental.pallas import tpu as pltpu

F32 = jnp.float32
BF16 = jnp.bfloat16

GLA_HEADS = 8
GLA_RANK = 16
GLA_TAU = 16.0
GLA_CHUNK = 64
DIL_HEADS = 32
DIL_HD = 128
LN_EPS = 1e-5
DEPTH = 2
DEEPNORM_ALPHA = (2 * DEPTH) ** 0.25
LOG2_E = 1.4426950408889634

V7X_VMEM_BYTES = 64 * 1024 * 1024
VMEM_LIMIT = V7X_VMEM_BYTES - 8 * 1024 * 1024

PERM_RES = 16
PERM_RUN = 128
PERM_SPAN = PERM_RES * PERM_RUN


def _cparams(n_axes):
    return pltpu.CompilerParams(
        dimension_semantics=("arbitrary",) * n_axes,
        vmem_limit_bytes=VMEM_LIMIT,
    )


def _mm_kernel(x_ref, w_ref, o_ref, *, act):
    w = w_ref[...].astype(BF16)
    acc = jnp.dot(x_ref[...], w, preferred_element_type=F32)
    if act == "relu2":
        acc = jnp.square(jnp.maximum(acc, 0.0))
    o_ref[...] = acc.astype(o_ref.dtype)


def _matmul(x, w, layer, *, out_dtype, act=None, tm=2048, tn=512, name):
    m, k = x.shape
    n = w.shape[-1]
    assert m % tm == 0 and n % tn == 0 and w.shape[-2] == k
    if w.ndim == 3:
        w_spec = pl.BlockSpec((None, k, tn), lambda i, j: (layer, 0, j))
    else:
        w_spec = pl.BlockSpec((k, tn), lambda i, j: (0, j))
    x_spec = pl.BlockSpec((tm, k), lambda i, j: (i, 0), pipeline_mode=pl.Buffered(1))
    return pl.pallas_call(
        functools.partial(_mm_kernel, act=act),
        grid=(m // tm, n // tn),
        in_specs=[x_spec, w_spec],
        out_specs=pl.BlockSpec((tm, tn), lambda i, j: (i, j)),
        out_shape=jax.ShapeDtypeStruct((m, n), out_dtype),
        compiler_params=_cparams(2),
        name=name,
    )(x, w)


def _mm_acc_kernel(x_ref, w_ref, o_ref, *, n_chunk):
    @pl.when(pl.program_id(2) == 0)
    def _():
        o_ref[...] = jnp.zeros_like(o_ref)

    x = x_ref[...]
    for n0 in range(0, o_ref.shape[1], n_chunk):
        w = w_ref[:, n0:n0 + n_chunk].astype(BF16)
        o_ref[:, n0:n0 + n_chunk] += jnp.dot(x, w, preferred_element_type=F32)


def _matmul_ktiled(x, w, layer, *, tm=1024, tn=2048, tk=1024, n_chunk=256, name):
    m, k = x.shape
    n = w.shape[-1]
    assert m % tm == 0 and n % tn == 0 and k % tk == 0 and tn % n_chunk == 0
    return pl.pallas_call(
        functools.partial(_mm_acc_kernel, n_chunk=n_chunk),
        grid=(n // tn, m // tm, k // tk),
        in_specs=[
            pl.BlockSpec((tm, tk), lambda j, i, kk: (i, kk)),
            pl.BlockSpec((None, tk, tn), lambda j, i, kk: (layer, kk, j)),
        ],
        out_specs=pl.BlockSpec((tm, tn), lambda j, i, kk: (i, j)),
        out_shape=jax.ShapeDtypeStruct((m, n), F32),
        compiler_params=_cparams(3),
        name=name,
    )(x, w)


def _gate_kernel(x_ref, w1_ref, w2_ref, b_ref, o_ref):
    t = jnp.dot(x_ref[...], w1_ref[...].astype(BF16), preferred_element_type=F32)
    z = jnp.dot(t.astype(BF16), w2_ref[...].astype(BF16), preferred_element_type=F32)
    z = z + b_ref[...]
    log_sig = jnp.minimum(z, 0.0) - jnp.log1p(jnp.exp(-jnp.abs(z)))
    o_ref[...] = log_sig / GLA_TAU


def _gate(x, w1, w2, b, *, tm=512):
    m, k = x.shape
    rank = w1.shape[-1]
    n = w2.shape[-1]
    return pl.pallas_call(
        _gate_kernel,
        grid=(m // tm,),
        in_specs=[
            pl.BlockSpec((tm, k), lambda i: (i, 0)),
            pl.BlockSpec((k, rank), lambda i: (0, 0)),
            pl.BlockSpec((rank, n), lambda i: (0, 0)),
            pl.BlockSpec((1, n), lambda i: (0, 0)),
        ],
        out_specs=pl.BlockSpec((tm, n), lambda i: (i, 0)),
        out_shape=jax.ShapeDtypeStruct((m, n), F32),
        compiler_params=_cparams(1),
        name="gla_gate",
    )(x, w1, w2, b)


def _cumsum_rows(tri_bf16, g):
    g_hi = g.astype(BF16)
    r1 = g - g_hi.astype(F32)
    g_mid = r1.astype(BF16)
    g_lo = (r1 - g_mid.astype(F32)).astype(BF16)
    out = jnp.dot(tri_bf16, g_hi, preferred_element_type=F32)
    out = out + jnp.dot(tri_bf16, g_mid, preferred_element_type=F32)
    out = out + jnp.dot(tri_bf16, g_lo, preferred_element_type=F32)
    return out


def _gla_kernel(q_ref, k_ref, v_ref, r_ref, g_ref, gng_ref, gnb_ref, y_ref,
                state_ref, o_scr, *, dk, chunks):
    @pl.when(pl.program_id(2) == 0)
    def _():
        state_ref[...] = jnp.zeros_like(state_ref)

    c_len = GLA_CHUNK
    row = lax.broadcasted_iota(jnp.int32, (c_len, c_len), 0)
    col = lax.broadcasted_iota(jnp.int32, (c_len, c_len), 1)
    causal = row >= col
    tri = causal.astype(BF16)
    nt = (((1,), (1,)), ((), ()))
    tn = (((0,), (0,)), ((), ()))

    for c in range(chunks):
        sl = slice(c * c_len, (c + 1) * c_len)
        bcum = _cumsum_rows(tri, g_ref[sl, :])
        b_mid = bcum[c_len // 2 - 1:c_len // 2, :]
        b_last = bcum[c_len - 1:c_len, :]
        qc = q_ref[sl, :] * (dk ** -0.5)
        kc = k_ref[sl, :]
        vc = v_ref[sl, :].astype(BF16)
        qa = (qc * jnp.exp(bcum - b_mid)).astype(BF16)
        ka = (kc * jnp.exp(b_mid - bcum)).astype(BF16)
        attn = lax.dot_general(qa, ka, nt, preferred_element_type=F32)
        attn = jnp.where(causal, attn, 0.0)
        o_intra = jnp.dot(attn.astype(BF16), vc, preferred_element_type=F32)
        qs = (qc * jnp.exp(bcum)).astype(BF16)
        ks = (kc * jnp.exp(b_last - bcum)).astype(BF16)
        st = state_ref[...]
        o_inter = lax.dot_general(qs, st.astype(BF16), nt, preferred_element_type=F32)
        upd = lax.dot_general(vc, ks, tn, preferred_element_type=F32)
        state_ref[...] = jnp.exp(b_last) * st + upd
        o_scr[sl, :] = o_intra + o_inter

    o = o_scr[...]
    mu = jnp.mean(o, axis=-1, keepdims=True)
    var = jnp.mean(jnp.square(o - mu), axis=-1, keepdims=True)
    o = (o - mu) * lax.rsqrt(var + LN_EPS)
    o = o * gng_ref[...] + gnb_ref[...]
    r = r_ref[...]
    silu = r * (1.0 / (1.0 + jnp.exp(-r)))
    y_ref[...] = (silu * o).astype(y_ref.dtype)


def _gla(proj, g, gn_g, gn_b, *, bsz, s_len, heads, tb=512):
    t_len = proj.shape[0]
    dk = g.shape[1] // heads
    dv = gn_g.shape[1] // heads
    assert proj.shape[1] == 2 * heads * dk + 2 * heads * dv and dv == 2 * dk
    assert s_len % tb == 0 and tb % GLA_CHUNK == 0
    nblk = s_len // tb
    rows = lambda b, h, i: b * nblk + i
    kernel = functools.partial(_gla_kernel, dk=dk, chunks=tb // GLA_CHUNK)
    return pl.pallas_call(
        kernel,
        grid=(bsz, heads, nblk),
        in_specs=[
            pl.BlockSpec((tb, dk), lambda b, h, i: (rows(b, h, i), h)),
            pl.BlockSpec((tb, dk), lambda b, h, i: (rows(b, h, i), heads + h)),
            pl.BlockSpec((tb, dv), lambda b, h, i: (rows(b, h, i), heads + h)),
            pl.BlockSpec((tb, dv), lambda b, h, i: (rows(b, h, i), 2 * heads + h)),
            pl.BlockSpec((tb, dk), lambda b, h, i: (rows(b, h, i), h)),
            pl.BlockSpec((1, dv), lambda b, h, i: (0, h)),
            pl.BlockSpec((1, dv), lambda b, h, i: (0, h)),
        ],
        out_specs=pl.BlockSpec((tb, dv), lambda b, h, i: (rows(b, h, i), h)),
        out_shape=jax.ShapeDtypeStruct((t_len, heads * dv), BF16),
        scratch_shapes=[pltpu.VMEM((dv, dk), F32), pltpu.VMEM((tb, dv), F32)],
        compiler_params=_cparams(3),
        name="gla_mixer",
    )(proj, proj, proj, proj, g, gn_g, gn_b)


def _ln_kernel(h_ref, mix_ref, g_ref, b_ref, of_ref, ob_ref):
    z = DEEPNORM_ALPHA * h_ref[...] + mix_ref[...]
    mu = jnp.mean(z, axis=-1, keepdims=True)
    var = jnp.mean(jnp.square(z - mu), axis=-1, keepdims=True)
    y = (z - mu) * lax.rsqrt(var + LN_EPS)
    y = y * g_ref[...] + b_ref[...]
    of_ref[...] = y
    ob_ref[...] = y.astype(BF16)


def _res_ln(h, mix, gain, bias, *, tm=256, name):
    m, d = h.shape
    row_spec = pl.BlockSpec((tm, d), lambda i: (i, 0))
    vec_spec = pl.BlockSpec((1, d), lambda i: (0, 0))
    return pl.pallas_call(
        _ln_kernel,
        grid=(m // tm,),
        in_specs=[row_spec, row_spec, vec_spec, vec_spec],
        out_specs=[row_spec, row_spec],
        out_shape=[jax.ShapeDtypeStruct((m, d), F32), jax.ShapeDtypeStruct((m, d), BF16)],
        compiler_params=_cparams(1),
        name=name,
    )(h, mix, gain.reshape(1, d), bias.reshape(1, d))


def _dilated_kernel(q1_ref, q4_ref, q16_ref, k1_ref, k4_ref, k16_ref,
                    v1_ref, v4_ref, v16_ref, o_ref, o1_scr, lse1_scr, o4_scr, lse4_scr,
                    *, spans, unroll):
    blk = 128
    scale = DIL_HD ** -0.5
    nt = (((1,), (1,)), ((), ()))
    ridx = lax.broadcasted_iota(jnp.int32, (blk, blk), 0)
    cidx = lax.broadcasted_iota(jnp.int32, (blk, blk), 1)

    def gather(ref, starts, run):
        if len(starts) == 1:
            return ref[pl.ds(starts[0], run), :]
        return jnp.concatenate([ref[pl.ds(s, run), :] for s in starts], axis=0)

    def scatter(ref, starts, run, val):
        for n, s in enumerate(starts):
            ref[pl.ds(s, run), :] = val[n * run:(n + 1) * run, :]

    def branch(q_ref, k_ref, v_ref, n_blocks, starts_fn, run, pos_fn, finish):
        pos_q = pos_fn(ridx)
        pos_k = pos_fn(cidx)
        mask = jnp.concatenate([pos_k >= pos_q, pos_k <= pos_q], axis=1)

        def one_block(n):
            cur, prev, has_prev = starts_fn(n)
            q = gather(q_ref, cur, run).astype(BF16)
            kk = jnp.concatenate([gather(k_ref, prev, run), gather(k_ref, cur, run)], axis=0)
            vv = jnp.concatenate([gather(v_ref, prev, run), gather(v_ref, cur, run)], axis=0)
            s = lax.dot_general(q, kk.astype(BF16), nt, preferred_element_type=F32)
            s = jnp.where(mask, s, -jnp.inf)
            no_prev = jnp.where(has_prev, 0.0, -jnp.inf)
            s = jnp.concatenate([s[:, :blk] + no_prev, s[:, blk:]], axis=1)
            m = jnp.max(s, axis=-1, keepdims=True)
            p = jnp.exp2((s - m) * (scale * LOG2_E))
            l = jnp.sum(p, axis=-1, keepdims=True)
            o = jnp.dot(p.astype(BF16), vv.astype(BF16), preferred_element_type=F32) / l
            lse = jnp.broadcast_to(m * scale + jnp.log(l), (blk, blk))
            finish(cur, run, o, lse)

        def body(it, carry):
            for u in range(unroll):
                one_block(it * unroll + u)
            return carry

        assert n_blocks % unroll == 0
        lax.fori_loop(0, n_blocks // unroll, body, 0)

    def al(x, mult):
        return pl.multiple_of(x, mult)

    def divmod_pow2(x, d):
        shift = d.bit_length() - 1
        assert d == 1 << shift
        return x >> shift, x & (d - 1)

    def keep(o_scr, lse_scr):
        def finish(cur, run, o, lse):
            scatter(o_scr, cur, run, o)
            scatter(lse_scr, cur, run, lse)
        return finish

    def starts_d1(n):
        def runs(b):
            span, sub = divmod_pow2(b, 16)
            return [al(span * PERM_SPAN + r * PERM_RUN + sub * 8, 8) for r in range(PERM_RES)]
        return runs(n), runs(jnp.maximum(n - 1, 0)), n > 0

    branch(q1_ref, k1_ref, v1_ref, spans * 16, starts_d1, 8,
           lambda i: 16 * (i & 7) + (i >> 3), keep(o1_scr, lse1_scr))

    def starts_d4(n):
        res, c = divmod_pow2(n, spans * 4)
        def runs(cc):
            span, sub = divmod_pow2(cc, 4)
            return [al(span * PERM_SPAN + (res + 4 * u) * PERM_RUN + sub * 32, 32) for u in range(4)]
        return runs(c), runs(jnp.maximum(c - 1, 0)), c > 0

    branch(q4_ref, k4_ref, v4_ref, spans * 16, starts_d4, 32,
           lambda i: 4 * (i & 31) + (i >> 5), keep(o4_scr, lse4_scr))

    def starts_d16(n):
        res, span = divmod_pow2(n, spans)
        def runs(sp):
            return [al(sp * PERM_SPAN + res * PERM_RUN, PERM_RUN)]
        return runs(span), runs(jnp.maximum(span - 1, 0)), span > 0

    def combine(cur, run, o16, lse16):
        lse1 = gather(lse1_scr, cur, run)
        lse4 = gather(lse4_scr, cur, run)
        top = jnp.maximum(jnp.maximum(lse1, lse4), lse16)
        e1 = jnp.exp(lse1 - top)
        e4 = jnp.exp(lse4 - top)
        e16 = jnp.exp(lse16 - top)
        den = e1 + e4 + e16
        o = (e1 / den) * gather(o1_scr, cur, run) + (e4 / den) * gather(o4_scr, cur, run)
        o = o + (e16 / den) * o16
        scatter(o_ref, cur, run, o.astype(o_ref.dtype))

    branch(q16_ref, k16_ref, v16_ref, spans * 16, starts_d16, 128, lambda i: i, combine)


def _dilated_attention(q, kv, *, bsz, s_len, heads, unroll=16):
    t_len = q.shape[0]
    hd = DIL_HD
    assert s_len % PERM_SPAN == 0
    spans = s_len // PERM_SPAN
    assert q.shape[1] == 3 * heads * hd and kv.shape[1] == 6 * heads * hd

    def qspec(g):
        return pl.BlockSpec((s_len, hd), lambda b, h: (b, g * heads + h))

    def kvspec(g, which):
        return pl.BlockSpec((s_len, hd), lambda b, h: (b, (2 * g + which) * heads + h))

    return pl.pallas_call(
        functools.partial(_dilated_kernel, spans=spans, unroll=unroll),
        grid=(bsz, heads),
        in_specs=[qspec(0), qspec(1), qspec(2),
                  kvspec(0, 0), kvspec(1, 0), kvspec(2, 0),
                  kvspec(0, 1), kvspec(1, 1), kvspec(2, 1)],
        out_specs=pl.BlockSpec((s_len, hd), lambda b, h: (b, h)),
        out_shape=jax.ShapeDtypeStruct((t_len, heads * hd), BF16),
        scratch_shapes=[pltpu.VMEM((s_len, hd), F32)] * 4,
        compiler_params=_cparams(2),
        name="dilated_attention",
    )(q, q, q, kv, kv, kv, kv, kv, kv)


def _to_perm(x, bsz, s_len):
    d = x.shape[-1]
    x = x.reshape(bsz, s_len // PERM_SPAN, PERM_RUN, PERM_RES, d)
    return x.transpose(0, 1, 3, 2, 4).reshape(bsz * s_len, d)


def _from_perm(x, bsz, s_len):
    d = x.shape[-1]
    x = x.reshape(bsz, s_len // PERM_SPAN, PERM_RES, PERM_RUN, d)
    return x.transpose(0, 1, 3, 2, 4).reshape(bsz, s_len, d)


def _mlp_block(h, hb, w1, w2, ln_g, ln_b, layer):
    hmid = _matmul(hb, w1, layer, out_dtype=BF16, act="relu2", name=f"mlp_up_{layer}")
    mix = _matmul_ktiled(hmid, w2, layer, name=f"mlp_down_{layer}")
    return _res_ln(h, mix, ln_g[layer, 1], ln_b[layer, 1], name=f"ln_mlp_{layer}")


def kernel(x, a_w_in, a_w_g1, a_w_g2, a_b_g, a_gn_g, a_gn_b, a_w_out, b_w_q, kv_w, b_w_out,
           mlp_w1, mlp_w2, ln_g, ln_b):
    bsz, s_len, d = x.shape
    t_len = bsz * s_len
    h = x.reshape(t_len, d)
    hb = h.astype(BF16)

    proj = _matmul(hb, a_w_in, 0, out_dtype=F32, name="gla_in_proj")
    g = _gate(hb, a_w_g1[0], a_w_g2[0], a_b_g[0].reshape(1, -1))
    y = _gla(proj, g, a_gn_g[0].reshape(1, -1), a_gn_b[0].reshape(1, -1),
             bsz=bsz, s_len=s_len, heads=GLA_HEADS)
    mix = _matmul(y, a_w_out, 0, out_dtype=F32, name="gla_out_proj")
    h, hb = _res_ln(h, mix, ln_g[0, 0], ln_b[0, 0], name="ln_mix_0")
    h, hb = _mlp_block(h, hb, mlp_w1, mlp_w2, ln_g, ln_b, 0)

    h = _to_perm(h, bsz, s_len)
    hb = h.astype(BF16)
    q = _matmul(hb, b_w_q, 0, out_dtype=F32, name="dil_q_proj")
    kv = _matmul(hb, kv_w, 0, out_dtype=F32, name="dil_kv_proj")
    o = _dilated_attention(q, kv, bsz=bsz, s_len=s_len, heads=DIL_HEADS)
    mix = _matmul(o, b_w_out, 0, out_dtype=F32, name="dil_out_proj")
    h, hb = _res_ln(h, mix, ln_g[1, 0], ln_b[1, 0], name="ln_mix_1")
    h, hb = _mlp_block(h, hb, mlp_w1, mlp_w2, ln_g, ln_b, 1)
    return _from_perm(h, bsz, s_len)
```

```python
import functools

import jax
import jax.numpy as jnp
from jax import lax
from jax.experimental import pallas as pl
from jax.experimental.pallas import tpu as pltpu

F32 = jnp.float32
BF16 = jnp.bfloat16

GLA_HEADS = 8
GLA_RANK = 16
GLA_TAU = 16.0
GLA_CHUNK = 64
DIL_HEADS = 32
DIL_HD = 128
LN_EPS = 1e-5
DEPTH = 2
DEEPNORM_ALPHA = (2 * DEPTH) ** 0.25
LOG2_E = 1.4426950408889634

V7X_VMEM_BYTES = 64 * 1024 * 1024
VMEM_LIMIT = V7X_VMEM_BYTES - 8 * 1024 * 1024

PERM_RES = 16
PERM_RUN = 128
PERM_SPAN = PERM_RES * PERM_RUN


def _cparams(n_axes):
    return pltpu.CompilerParams(
        dimension_semantics=("arbitrary",) * n_axes,
        vmem_limit_bytes=VMEM_LIMIT,
    )


def _mm_kernel(x_ref, w_ref, o_ref, *, act):
    w = w_ref[...].astype(BF16)
    acc = jnp.dot(x_ref[...], w, preferred_element_type=F32)
    if act == "relu2":
        acc = jnp.square(jnp.maximum(acc, 0.0))
    o_ref[...] = acc.astype(o_ref.dtype)


def _matmul(x, w, layer, *, out_dtype, act=None, tm=2048, tn=512, x_buffers=1, name):
    m, k = x.shape
    n = w.shape[-1]
    assert m % tm == 0 and n % tn == 0 and w.shape[-2] == k
    if w.ndim == 3:
        w_spec = pl.BlockSpec((None, k, tn), lambda i, j: (layer, 0, j))
    else:
        w_spec = pl.BlockSpec((k, tn), lambda i, j: (0, j))
    x_spec = pl.BlockSpec((tm, k), lambda i, j: (i, 0), pipeline_mode=pl.Buffered(x_buffers))
    return pl.pallas_call(
        functools.partial(_mm_kernel, act=act),
        grid=(m // tm, n // tn),
        in_specs=[x_spec, w_spec],
        out_specs=pl.BlockSpec((tm, tn), lambda i, j: (i, j)),
        out_shape=jax.ShapeDtypeStruct((m, n), out_dtype),
        compiler_params=_cparams(2),
        name=name,
    )(x, w)


def _mm_acc_kernel(x_ref, w_ref, o_ref, *, n_chunk):
    @pl.when(pl.program_id(2) == 0)
    def _():
        o_ref[...] = jnp.zeros_like(o_ref)

    for n0 in range(0, o_ref.shape[1], n_chunk):
        w = w_ref[:, n0:n0 + n_chunk].astype(BF16)
        o_ref[:, n0:n0 + n_chunk] += jnp.dot(x_ref[...], w, preferred_element_type=F32)


def _matmul_ktiled(x, w, layer, *, tm=2048, tn=1024, tk=2048, n_chunk=256, name):
    m, k = x.shape
    n = w.shape[-1]
    assert m % tm == 0 and n % tn == 0 and k % tk == 0 and tn % n_chunk == 0
    return pl.pallas_call(
        functools.partial(_mm_acc_kernel, n_chunk=n_chunk),
        grid=(n // tn, m // tm, k // tk),
        in_specs=[
            pl.BlockSpec((tm, tk), lambda j, i, kk: (i, kk)),
            pl.BlockSpec((None, tk, tn), lambda j, i, kk: (layer, kk, j)),
        ],
        out_specs=pl.BlockSpec((tm, tn), lambda j, i, kk: (i, j)),
        out_shape=jax.ShapeDtypeStruct((m, n), F32),
        compiler_params=_cparams(3),
        name=name,
    )(x, w)


def _gate_kernel(x_ref, w1_ref, w2_ref, b_ref, o_ref, xb_ref):
    xb = x_ref[...].astype(BF16)
    xb_ref[...] = xb
    t = jnp.dot(xb, w1_ref[...].astype(BF16), preferred_element_type=F32)
    z = jnp.dot(t.astype(BF16), w2_ref[...].astype(BF16), preferred_element_type=F32)
    z = z + b_ref[...]
    log_sig = jnp.minimum(z, 0.0) - jnp.log1p(jnp.exp(-jnp.abs(z)))
    o_ref[...] = log_sig / GLA_TAU


def _gate(x, w1, w2, b, *, tm=512):
    m, k = x.shape
    rank = w1.shape[-1]
    n = w2.shape[-1]
    return pl.pallas_call(
        _gate_kernel,
        grid=(m // tm,),
        in_specs=[
            pl.BlockSpec((tm, k), lambda i: (i, 0)),
            pl.BlockSpec((k, rank), lambda i: (0, 0)),
            pl.BlockSpec((rank, n), lambda i: (0, 0)),
            pl.BlockSpec((1, n), lambda i: (0, 0)),
        ],
        out_specs=[pl.BlockSpec((tm, n), lambda i: (i, 0)),
                   pl.BlockSpec((tm, k), lambda i: (i, 0))],
        out_shape=[jax.ShapeDtypeStruct((m, n), F32), jax.ShapeDtypeStruct((m, k), BF16)],
        compiler_params=_cparams(1),
        name="gla_gate",
    )(x, w1, w2, b)


def _cumsum_rows(tri_bf16, g):
    g_hi = g.astype(BF16)
    r1 = g - g_hi.astype(F32)
    g_mid = r1.astype(BF16)
    g_lo = (r1 - g_mid.astype(F32)).astype(BF16)
    out = jnp.dot(tri_bf16, g_hi, preferred_element_type=F32)
    out = out + jnp.dot(tri_bf16, g_mid, preferred_element_type=F32)
    out = out + jnp.dot(tri_bf16, g_lo, preferred_element_type=F32)
    return out


def _gla_kernel(q_ref, k_ref, v_ref, r_ref, g_ref, gng_ref, gnb_ref, y_ref,
                state_ref, o_scr, *, dk, chunks):
    @pl.when(pl.program_id(2) == 0)
    def _():
        state_ref[...] = jnp.zeros_like(state_ref)

    c_len = GLA_CHUNK
    row = lax.broadcasted_iota(jnp.int32, (c_len, c_len), 0)
    col = lax.broadcasted_iota(jnp.int32, (c_len, c_len), 1)
    causal = row >= col
    tri = causal.astype(BF16)
    nt = (((1,), (1,)), ((), ()))
    tn = (((0,), (0,)), ((), ()))

    for c in range(chunks):
        sl = slice(c * c_len, (c + 1) * c_len)
        bcum = _cumsum_rows(tri, g_ref[sl, :])
        b_mid = bcum[c_len // 2 - 1:c_len // 2, :]
        b_last = bcum[c_len - 1:c_len, :]
        qc = q_ref[sl, :] * (dk ** -0.5)
        kc = k_ref[sl, :]
        vc = v_ref[sl, :].astype(BF16)
        qa = (qc * jnp.exp(bcum - b_mid)).astype(BF16)
        ka = (kc * jnp.exp(b_mid - bcum)).astype(BF16)
        attn = lax.dot_general(qa, ka, nt, preferred_element_type=F32)
        attn = jnp.where(causal, attn, 0.0)
        o_intra = jnp.dot(attn.astype(BF16), vc, preferred_element_type=F32)
        qs = (qc * jnp.exp(bcum)).astype(BF16)
        ks = (kc * jnp.exp(b_last - bcum)).astype(BF16)
        st = state_ref[...]
        o_inter = lax.dot_general(qs, st.astype(BF16), nt, preferred_element_type=F32)
        upd = lax.dot_general(vc, ks, tn, preferred_element_type=F32)
        state_ref[...] = jnp.exp(b_last) * st + upd
        o_scr[sl, :] = o_intra + o_inter

    o = o_scr[...]
    mu = jnp.mean(o, axis=-1, keepdims=True)
    var = jnp.mean(jnp.square(o - mu), axis=-1, keepdims=True)
    o = (o - mu) * lax.rsqrt(var + LN_EPS)
    o = o * gng_ref[...] + gnb_ref[...]
    r = r_ref[...]
    silu = r * (1.0 / (1.0 + jnp.exp(-r)))
    y_ref[...] = (silu * o).astype(y_ref.dtype)


def _gla(proj, g, gn_g, gn_b, *, bsz, s_len, heads, tb=1024):
    t_len = proj.shape[0]
    dk = g.shape[1] // heads
    dv = gn_g.shape[1] // heads
    assert proj.shape[1] == 2 * heads * dk + 2 * heads * dv and dv == 2 * dk
    assert s_len % tb == 0 and tb % GLA_CHUNK == 0
    nblk = s_len // tb
    rows = lambda b, h, i: b * nblk + i
    kernel = functools.partial(_gla_kernel, dk=dk, chunks=tb // GLA_CHUNK)
    return pl.pallas_call(
        kernel,
        grid=(bsz, heads, nblk),
        in_specs=[
            pl.BlockSpec((tb, dk), lambda b, h, i: (rows(b, h, i), h)),
            pl.BlockSpec((tb, dk), lambda b, h, i: (rows(b, h, i), heads + h)),
            pl.BlockSpec((tb, dv), lambda b, h, i: (rows(b, h, i), heads + h)),
            pl.BlockSpec((tb, dv), lambda b, h, i: (rows(b, h, i), 2 * heads + h)),
            pl.BlockSpec((tb, dk), lambda b, h, i: (rows(b, h, i), h)),
            pl.BlockSpec((1, dv), lambda b, h, i: (0, h)),
            pl.BlockSpec((1, dv), lambda b, h, i: (0, h)),
        ],
        out_specs=pl.BlockSpec((tb, dv), lambda b, h, i: (rows(b, h, i), h)),
        out_shape=jax.ShapeDtypeStruct((t_len, heads * dv), BF16),
        scratch_shapes=[pltpu.VMEM((dv, dk), F32), pltpu.VMEM((tb, dv), F32)],
        compiler_params=_cparams(3),
        name="gla_mixer",
    )(proj, proj, proj, proj, g, gn_g, gn_b)


def _ln_kernel(h_ref, mix_ref, g_ref, b_ref, of_ref, *maybe_ob_ref):
    z = DEEPNORM_ALPHA * h_ref[...] + mix_ref[...]
    mu = jnp.mean(z, axis=-1, keepdims=True)
    var = jnp.mean(jnp.square(z - mu), axis=-1, keepdims=True)
    y = (z - mu) * lax.rsqrt(var + LN_EPS)
    y = y * g_ref[...] + b_ref[...]
    of_ref[...] = y
    for ob_ref in maybe_ob_ref:
        ob_ref[...] = y.astype(BF16)


def _res_ln(h, mix, gain, bias, *, with_bf16=True, tm=256, name):
    m, d = h.shape
    row_spec = pl.BlockSpec((tm, d), lambda i: (i, 0))
    vec_spec = pl.BlockSpec((1, d), lambda i: (0, 0))
    out_shape = [jax.ShapeDtypeStruct((m, d), F32)]
    if with_bf16:
        out_shape.append(jax.ShapeDtypeStruct((m, d), BF16))
    return pl.pallas_call(
        _ln_kernel,
        grid=(m // tm,),
        in_specs=[row_spec, row_spec, vec_spec, vec_spec],
        out_specs=[row_spec] * len(out_shape),
        out_shape=out_shape,
        compiler_params=_cparams(1),
        name=name,
    )(h, mix, gain.reshape(1, d), bias.reshape(1, d))


def _dilated_kernel(q1_ref, q4_ref, q16_ref, k1_ref, k4_ref, k16_ref,
                    v1_ref, v4_ref, v16_ref, o_ref, o1_scr, lse1_scr, o4_scr, lse4_scr,
                    otok_scr, *, spans, unroll):
    blk = 128
    scale = DIL_HD ** -0.5
    nt = (((1,), (1,)), ((), ()))
    ridx = lax.broadcasted_iota(jnp.int32, (blk, blk), 0)
    cidx = lax.broadcasted_iota(jnp.int32, (blk, blk), 1)

    def gather(ref, starts, run):
        if len(starts) == 1:
            return ref[pl.ds(starts[0], run), :]
        return jnp.concatenate([ref[pl.ds(s, run), :] for s in starts], axis=0)

    def scatter(ref, starts, run, val):
        for n, s in enumerate(starts):
            ref[pl.ds(s, run), :] = val[n * run:(n + 1) * run, :]

    def branch(q_ref, k_ref, v_ref, n_blocks, starts_fn, run, pos_fn, finish):
        pos_q = pos_fn(ridx)
        pos_k = pos_fn(cidx)
        mask = jnp.concatenate([pos_k >= pos_q, pos_k <= pos_q], axis=1)

        def one_block(n):
            cur, prev, has_prev = starts_fn(n)
            q = gather(q_ref, cur, run).astype(BF16)
            kk = jnp.concatenate([gather(k_ref, prev, run), gather(k_ref, cur, run)], axis=0)
            vv = jnp.concatenate([gather(v_ref, prev, run), gather(v_ref, cur, run)], axis=0)
            s = lax.dot_general(q, kk.astype(BF16), nt, preferred_element_type=F32)
            s = jnp.where(mask, s, -jnp.inf)
            no_prev = jnp.where(has_prev, 0.0, -jnp.inf)
            s = jnp.concatenate([s[:, :blk] + no_prev, s[:, blk:]], axis=1)
            m = jnp.max(s, axis=-1, keepdims=True)
            p = jnp.exp2((s - m) * (scale * LOG2_E))
            l = jnp.sum(p, axis=-1, keepdims=True)
            o = jnp.dot(p.astype(BF16), vv.astype(BF16), preferred_element_type=F32) / l
            lse = jnp.broadcast_to(m * scale + jnp.log(l), (blk, blk))
            finish(n, cur, run, o, lse)

        def body(it, carry):
            for u in range(unroll):
                one_block(it * unroll + u)
            return carry

        assert n_blocks % unroll == 0
        lax.fori_loop(0, n_blocks // unroll, body, 0)

    def al(x, mult):
        return pl.multiple_of(x, mult)

    def divmod_pow2(x, d):
        shift = d.bit_length() - 1
        assert d == 1 << shift
        return x >> shift, x & (d - 1)

    def keep(o_scr, lse_scr):
        def finish(n, cur, run, o, lse):
            scatter(o_scr, cur, run, o)
            scatter(lse_scr, cur, run, lse)
        return finish

    def starts_d1(n):
        def runs(b):
            span, sub = divmod_pow2(b, 16)
            return [al(span * PERM_SPAN + r * PERM_RUN + sub * 8, 8) for r in range(PERM_RES)]
        return runs(n), runs(jnp.maximum(n - 1, 0)), n > 0

    branch(q1_ref, k1_ref, v1_ref, spans * 16, starts_d1, 8,
           lambda i: 16 * (i & 7) + (i >> 3), keep(o1_scr, lse1_scr))

    def starts_d4(n):
        res, c = divmod_pow2(n, spans * 4)
        def runs(cc):
            span, sub = divmod_pow2(cc, 4)
            return [al(span * PERM_SPAN + (res + 4 * u) * PERM_RUN + sub * 32, 32) for u in range(4)]
        return runs(c), runs(jnp.maximum(c - 1, 0)), c > 0

    branch(q4_ref, k4_ref, v4_ref, spans * 16, starts_d4, 32,
           lambda i: 4 * (i & 31) + (i >> 5), keep(o4_scr, lse4_scr))

    def starts_d16(n):
        res, span = divmod_pow2(n, spans)
        def runs(sp):
            return [al(sp * PERM_SPAN + res * PERM_RUN, PERM_RUN)]
        return runs(span), runs(jnp.maximum(span - 1, 0)), span > 0

    def combine(n, cur, run, o16, lse16):
        lse1 = gather(lse1_scr, cur, run)
        lse4 = gather(lse4_scr, cur, run)
        top = jnp.maximum(jnp.maximum(lse1, lse4), lse16)
        e1 = jnp.exp(lse1 - top)
        e4 = jnp.exp(lse4 - top)
        e16 = jnp.exp(lse16 - top)
        den = e1 + e4 + e16
        o = (e1 / den) * gather(o1_scr, cur, run) + (e4 / den) * gather(o4_scr, cur, run)
        o = o + (e16 / den) * o16
        res, span = divmod_pow2(n, spans)
        otok_scr[pl.ds(span * PERM_SPAN + res, PERM_RUN, stride=PERM_RES), :] = o

    branch(q16_ref, k16_ref, v16_ref, spans * 16, starts_d16, 128, lambda i: i, combine)
    o_ref[...] = otok_scr[...].astype(o_ref.dtype)


def _dilated_attention(q, kv, *, bsz, s_len, heads, unroll=16):
    t_len = q.shape[0]
    hd = DIL_HD
    assert s_len % PERM_SPAN == 0
    spans = s_len // PERM_SPAN
    assert q.shape[1] == 3 * heads * hd and kv.shape[1] == 6 * heads * hd

    def qspec(g):
        return pl.BlockSpec((s_len, hd), lambda b, h: (b, g * heads + h))

    def kvspec(g, which):
        return pl.BlockSpec((s_len, hd), lambda b, h: (b, (2 * g + which) * heads + h))

    return pl.pallas_call(
        functools.partial(_dilated_kernel, spans=spans, unroll=unroll),
        grid=(bsz, heads),
        in_specs=[qspec(0), qspec(1), qspec(2),
                  kvspec(0, 0), kvspec(1, 0), kvspec(2, 0),
                  kvspec(0, 1), kvspec(1, 1), kvspec(2, 1)],
        out_specs=pl.BlockSpec((s_len, hd), lambda b, h: (b, h)),
        out_shape=jax.ShapeDtypeStruct((t_len, heads * hd), BF16),
        scratch_shapes=[pltpu.VMEM((s_len, hd), F32)] * 5,
        compiler_params=_cparams(2),
        name="dilated_attention",
    )(q, q, q, kv, kv, kv, kv, kv, kv)


def _to_perm(x, bsz, s_len):
    d = x.shape[-1]
    x = x.reshape(bsz, s_len // PERM_SPAN, PERM_RUN, PERM_RES, d)
    return x.transpose(0, 1, 3, 2, 4).reshape(bsz * s_len, d)


def _mlp_block(h, hb, w1, w2, ln_g, ln_b, layer, *, with_bf16):
    hmid = _matmul(hb, w1, layer, out_dtype=BF16, act="relu2", name=f"mlp_up_{layer}")
    mix = _matmul_ktiled(hmid, w2, layer, name=f"mlp_down_{layer}")
    return _res_ln(h, mix, ln_g[layer, 1], ln_b[layer, 1], with_bf16=with_bf16,
                   name=f"ln_mlp_{layer}")


def kernel(x, a_w_in, a_w_g1, a_w_g2, a_b_g, a_gn_g, a_gn_b, a_w_out, b_w_q, kv_w, b_w_out,
           mlp_w1, mlp_w2, ln_g, ln_b):
    bsz, s_len, d = x.shape
    t_len = bsz * s_len
    h = x.reshape(t_len, d)

    g, hb = _gate(h, a_w_g1[0], a_w_g2[0], a_b_g[0].reshape(1, -1))
    proj = _matmul(hb, a_w_in, 0, out_dtype=F32, name="gla_in_proj")
    y = _gla(proj, g, a_gn_g[0].reshape(1, -1), a_gn_b[0].reshape(1, -1),
             bsz=bsz, s_len=s_len, heads=GLA_HEADS)
    mix = _matmul(y, a_w_out, 0, out_dtype=F32, tm=1024, x_buffers=2, name="gla_out_proj")
    h, hb = _res_ln(h, mix, ln_g[0, 0], ln_b[0, 0], name="ln_mix_0")
    h, hb = _mlp_block(h, hb, mlp_w1, mlp_w2, ln_g, ln_b, 0, with_bf16=True)

    hb_perm = _to_perm(hb, bsz, s_len)
    q = _matmul(hb_perm, b_w_q, 0, out_dtype=F32, name="dil_q_proj")
    kv = _matmul(hb_perm, kv_w, 0, out_dtype=F32, name="dil_kv_proj")
    o = _dilated_attention(q, kv, bsz=bsz, s_len=s_len, heads=DIL_HEADS)
    mix = _matmul(o, b_w_out, 0, out_dtype=F32, tm=1024, x_buffers=2, name="dil_out_proj")
    h, hb = _res_ln(h, mix, ln_g[1, 0], ln_b[1, 0], name="ln_mix_1")
    (h,) = _mlp_block(h, hb, mlp_w1, mlp_w2, ln_g, ln_b, 1, with_bf16=False)
    return h.reshape(bsz, s_len, d)
```

```python
import functools

import jax
import jax.numpy as jnp
from jax import lax
from jax.experimental import pallas as pl
from jax.experimental.pallas import tpu as pltpu

F32 = jnp.float32
BF16 = jnp.bfloat16

GLA_HEADS = 8
GLA_RANK = 16
GLA_TAU = 16.0
GLA_CHUNK = 64
DIL_HEADS = 32
DIL_HD = 128
LN_EPS = 1e-5
DEPTH = 2
DEEPNORM_ALPHA = (2 * DEPTH) ** 0.25
LOG2_E = 1.4426950408889634

V7X_VMEM_BYTES = 64 * 1024 * 1024
VMEM_LIMIT = V7X_VMEM_BYTES - 8 * 1024 * 1024

PERM_RES = 16
PERM_RUN = 128
PERM_SPAN = PERM_RES * PERM_RUN


def _cparams(n_axes):
    return pltpu.CompilerParams(
        dimension_semantics=("arbitrary",) * n_axes,
        vmem_limit_bytes=VMEM_LIMIT,
    )


def _mm_kernel(x_ref, w_ref, *rest, act, res_scale):
    o_ref = rest[-1]
    w = w_ref[...].astype(BF16)
    acc = jnp.dot(x_ref[...], w, preferred_element_type=F32)
    if act == "relu2":
        acc = jnp.square(jnp.maximum(acc, 0.0))
    if res_scale is not None:
        acc = res_scale * rest[0][...] + acc
    o_ref[...] = acc.astype(o_ref.dtype)


def _matmul(x, w, layer, *, out_dtype, act=None, residual=None, res_scale=None,
            tm=2048, tn=512, x_buffers=1, name):
    m, k = x.shape
    n = w.shape[-1]
    assert m % tm == 0 and n % tn == 0 and w.shape[-2] == k
    assert (residual is None) == (res_scale is None)
    if w.ndim == 3:
        w_spec = pl.BlockSpec((None, k, tn), lambda i, j: (layer, 0, j))
    else:
        w_spec = pl.BlockSpec((k, tn), lambda i, j: (0, j))
    x_spec = pl.BlockSpec((tm, k), lambda i, j: (i, 0), pipeline_mode=pl.Buffered(x_buffers))
    tile_spec = pl.BlockSpec((tm, tn), lambda i, j: (i, j))
    operands, in_specs = [x, w], [x_spec, w_spec]
    if residual is not None:
        operands.append(residual)
        in_specs.append(tile_spec)
    return pl.pallas_call(
        functools.partial(_mm_kernel, act=act, res_scale=res_scale),
        grid=(m // tm, n // tn),
        in_specs=in_specs,
        out_specs=tile_spec,
        out_shape=jax.ShapeDtypeStruct((m, n), out_dtype),
        compiler_params=_cparams(2),
        name=name,
    )(*operands)


def _mm_acc_kernel(x_ref, w_ref, o_ref, *, n_chunk):
    @pl.when(pl.program_id(2) == 0)
    def _():
        o_ref[...] = jnp.zeros_like(o_ref)

    for n0 in range(0, o_ref.shape[1], n_chunk):
        w = w_ref[:, n0:n0 + n_chunk].astype(BF16)
        o_ref[:, n0:n0 + n_chunk] += jnp.dot(x_ref[...], w, preferred_element_type=F32)


def _matmul_ktiled(x, w, layer, *, tm=2048, tn=1024, tk=2048, n_chunk=256, name):
    m, k = x.shape
    n = w.shape[-1]
    assert m % tm == 0 and n % tn == 0 and k % tk == 0 and tn % n_chunk == 0
    return pl.pallas_call(
        functools.partial(_mm_acc_kernel, n_chunk=n_chunk),
        grid=(n // tn, m // tm, k // tk),
        in_specs=[
            pl.BlockSpec((tm, tk), lambda j, i, kk: (i, kk)),
            pl.BlockSpec((None, tk, tn), lambda j, i, kk: (layer, kk, j)),
        ],
        out_specs=pl.BlockSpec((tm, tn), lambda j, i, kk: (i, j)),
        out_shape=jax.ShapeDtypeStruct((m, n), F32),
        compiler_params=_cparams(3),
        name=name,
    )(x, w)


def _gate_kernel(x_ref, w1_ref, w2_ref, b_ref, o_ref, xb_ref):
    xb = x_ref[...].astype(BF16)
    xb_ref[...] = xb
    t = jnp.dot(xb, w1_ref[...].astype(BF16), preferred_element_type=F32)
    z = jnp.dot(t.astype(BF16), w2_ref[...].astype(BF16), preferred_element_type=F32)
    z = z + b_ref[...]
    log_sig = jnp.minimum(z, 0.0) - jnp.log1p(jnp.exp(-jnp.abs(z)))
    o_ref[...] = log_sig / GLA_TAU


def _gate(x, w1, w2, b, *, tm=512):
    m, k = x.shape
    rank = w1.shape[-1]
    n = w2.shape[-1]
    return pl.pallas_call(
        _gate_kernel,
        grid=(m // tm,),
        in_specs=[
            pl.BlockSpec((tm, k), lambda i: (i, 0)),
            pl.BlockSpec((k, rank), lambda i: (0, 0)),
            pl.BlockSpec((rank, n), lambda i: (0, 0)),
            pl.BlockSpec((1, n), lambda i: (0, 0)),
        ],
        out_specs=[pl.BlockSpec((tm, n), lambda i: (i, 0)),
                   pl.BlockSpec((tm, k), lambda i: (i, 0))],
        out_shape=[jax.ShapeDtypeStruct((m, n), F32), jax.ShapeDtypeStruct((m, k), BF16)],
        compiler_params=_cparams(1),
        name="gla_gate",
    )(x, w1, w2, b)


def _cumsum_rows(tri_bf16, g):
    g_hi = g.astype(BF16)
    r1 = g - g_hi.astype(F32)
    g_mid = r1.astype(BF16)
    g_lo = (r1 - g_mid.astype(F32)).astype(BF16)
    out = jnp.dot(tri_bf16, g_hi, preferred_element_type=F32)
    out = out + jnp.dot(tri_bf16, g_mid, preferred_element_type=F32)
    out = out + jnp.dot(tri_bf16, g_lo, preferred_element_type=F32)
    return out


def _gla_kernel(q_ref, k_ref, v_ref, r_ref, g_ref, gng_ref, gnb_ref, y_ref,
                state_ref, o_scr, *, dk, chunks):
    nseq = q_ref.shape[0]

    @pl.when(pl.program_id(1) == 0)
    def _():
        state_ref[...] = jnp.zeros_like(state_ref)

    c_len = GLA_CHUNK
    row = lax.broadcasted_iota(jnp.int32, (c_len, c_len), 0)
    col = lax.broadcasted_iota(jnp.int32, (c_len, c_len), 1)
    causal = row >= col
    tri = causal.astype(BF16)
    nt = (((1,), (1,)), ((), ()))
    tn = (((0,), (0,)), ((), ()))

    for c in range(chunks):
        sl = slice(c * c_len, (c + 1) * c_len)
        for s in range(nseq):
            bcum = _cumsum_rows(tri, g_ref[s, sl, :])
            b_mid = bcum[c_len // 2 - 1:c_len // 2, :]
            b_last = bcum[c_len - 1:c_len, :]
            qc = q_ref[s, sl, :] * (dk ** -0.5)
            kc = k_ref[s, sl, :]
            vc = v_ref[s, sl, :].astype(BF16)
            qa = (qc * jnp.exp(bcum - b_mid)).astype(BF16)
            ka = (kc * jnp.exp(b_mid - bcum)).astype(BF16)
            attn = lax.dot_general(qa, ka, nt, preferred_element_type=F32)
            attn = jnp.where(causal, attn, 0.0)
            o_intra = jnp.dot(attn.astype(BF16), vc, preferred_element_type=F32)
            qs = (qc * jnp.exp(bcum)).astype(BF16)
            ks = (kc * jnp.exp(b_last - bcum)).astype(BF16)
            st = state_ref[s]
            o_inter = lax.dot_general(qs, st.astype(BF16), nt, preferred_element_type=F32)
            upd = lax.dot_general(vc, ks, tn, preferred_element_type=F32)
            state_ref[s] = jnp.exp(b_last) * st + upd
            o_scr[s, sl, :] = o_intra + o_inter

    for s in range(nseq):
        o = o_scr[s]
        mu = jnp.mean(o, axis=-1, keepdims=True)
        var = jnp.mean(jnp.square(o - mu), axis=-1, keepdims=True)
        o = (o - mu) * lax.rsqrt(var + LN_EPS)
        o = o * gng_ref[...] + gnb_ref[...]
        r = r_ref[s]
        silu = r * (1.0 / (1.0 + jnp.exp(-r)))
        y_ref[s] = (silu * o).astype(y_ref.dtype)


def _gla(proj, g, gn_g, gn_b, *, bsz, s_len, heads, tb=512):
    t_len = proj.shape[0]
    dk = g.shape[1] // heads
    dv = gn_g.shape[1] // heads
    assert proj.shape[1] == 2 * heads * dk + 2 * heads * dv and dv == 2 * dk
    assert s_len % tb == 0 and tb % GLA_CHUNK == 0
    proj3 = proj.reshape(bsz, s_len, proj.shape[1])
    g3 = g.reshape(bsz, s_len, g.shape[1])
    kernel = functools.partial(_gla_kernel, dk=dk, chunks=tb // GLA_CHUNK)
    y = pl.pallas_call(
        kernel,
        grid=(heads, s_len // tb),
        in_specs=[
            pl.BlockSpec((bsz, tb, dk), lambda h, i: (0, i, h)),
            pl.BlockSpec((bsz, tb, dk), lambda h, i: (0, i, heads + h)),
            pl.BlockSpec((bsz, tb, dv), lambda h, i: (0, i, heads + h)),
            pl.BlockSpec((bsz, tb, dv), lambda h, i: (0, i, 2 * heads + h)),
            pl.BlockSpec((bsz, tb, dk), lambda h, i: (0, i, h)),
            pl.BlockSpec((1, dv), lambda h, i: (0, h)),
            pl.BlockSpec((1, dv), lambda h, i: (0, h)),
        ],
        out_specs=pl.BlockSpec((bsz, tb, dv), lambda h, i: (0, i, h)),
        out_shape=jax.ShapeDtypeStruct((bsz, s_len, heads * dv), BF16),
        scratch_shapes=[pltpu.VMEM((bsz, dv, dk), F32), pltpu.VMEM((bsz, tb, dv), F32)],
        compiler_params=_cparams(2),
        name="gla_mixer",
    )(proj3, proj3, proj3, proj3, g3, gn_g, gn_b)
    return y.reshape(t_len, heads * dv)


def _layer_norm_rows(z, g_ref, b_ref):
    mu = jnp.mean(z, axis=-1, keepdims=True)
    var = jnp.mean(jnp.square(z - mu), axis=-1, keepdims=True)
    return (z - mu) * lax.rsqrt(var + LN_EPS) * g_ref[...] + b_ref[...]


def _ln_cast_kernel(z_ref, g_ref, b_ref, ob_ref):
    ob_ref[...] = _layer_norm_rows(z_ref[...], g_ref, b_ref).astype(ob_ref.dtype)


def _ln_res_ln_kernel(z_ref, mix_ref, g1_ref, b1_ref, g2_ref, b2_ref, *out_refs):
    h = _layer_norm_rows(z_ref[...], g1_ref, b1_ref)
    y = _layer_norm_rows(DEEPNORM_ALPHA * h + mix_ref[...], g2_ref, b2_ref)
    for o_ref in out_refs:
        o_ref[...] = y.astype(o_ref.dtype)


def _ln_specs(d, tm):
    return pl.BlockSpec((tm, d), lambda i: (i, 0)), pl.BlockSpec((1, d), lambda i: (0, 0))


def _ln_cast(z, gain, bias, *, tm=256, name):
    m, d = z.shape
    row_spec, vec_spec = _ln_specs(d, tm)
    return pl.pallas_call(
        _ln_cast_kernel,
        grid=(m // tm,),
        in_specs=[row_spec, vec_spec, vec_spec],
        out_specs=row_spec,
        out_shape=jax.ShapeDtypeStruct((m, d), BF16),
        compiler_params=_cparams(1),
        name=name,
    )(z, gain.reshape(1, d), bias.reshape(1, d))


def _ln_res_ln(z, mix, gain1, bias1, gain2, bias2, *, with_bf16, tm=256, name):
    m, d = z.shape
    row_spec, vec_spec = _ln_specs(d, tm)
    out_shape = [jax.ShapeDtypeStruct((m, d), F32)]
    if with_bf16:
        out_shape.append(jax.ShapeDtypeStruct((m, d), BF16))
    vecs = [v.reshape(1, d) for v in (gain1, bias1, gain2, bias2)]
    return pl.pallas_call(
        _ln_res_ln_kernel,
        grid=(m // tm,),
        in_specs=[row_spec, row_spec] + [vec_spec] * 4,
        out_specs=[row_spec] * len(out_shape),
        out_shape=out_shape,
        compiler_params=_cparams(1),
        name=name,
    )(z, mix, *vecs)


def _dilated_kernel(q1_ref, q4_ref, q16_ref, k1_ref, k4_ref, k16_ref,
                    v1_ref, v4_ref, v16_ref, o_ref, o1_scr, lse1_scr, o4_scr, lse4_scr,
                    otok_scr, *, spans, unroll):
    blk = 128
    scale = DIL_HD ** -0.5
    nt = (((1,), (1,)), ((), ()))
    ridx = lax.broadcasted_iota(jnp.int32, (blk, blk), 0)
    cidx = lax.broadcasted_iota(jnp.int32, (blk, blk), 1)

    def gather(ref, starts, run):
        if len(starts) == 1:
            return ref[pl.ds(starts[0], run), :]
        return jnp.concatenate([ref[pl.ds(s, run), :] for s in starts], axis=0)

    def scatter(ref, starts, run, val):
        for n, s in enumerate(starts):
            ref[pl.ds(s, run), :] = val[n * run:(n + 1) * run, :]

    def branch(q_ref, k_ref, v_ref, n_blocks, starts_fn, run, pos_fn, finish):
        pos_q = pos_fn(ridx)
        pos_k = pos_fn(cidx)
        mask = jnp.concatenate([pos_k >= pos_q, pos_k <= pos_q], axis=1)

        def one_block(n):
            cur, prev, has_prev = starts_fn(n)
            q = gather(q_ref, cur, run).astype(BF16)
            kk = jnp.concatenate([gather(k_ref, prev, run), gather(k_ref, cur, run)], axis=0)
            vv = jnp.concatenate([gather(v_ref, prev, run), gather(v_ref, cur, run)], axis=0)
            s = lax.dot_general(q, kk.astype(BF16), nt, preferred_element_type=F32)
            s = jnp.where(mask, s, -jnp.inf)
            no_prev = jnp.where(has_prev, 0.0, -jnp.inf)
            s = jnp.concatenate([s[:, :blk] + no_prev, s[:, blk:]], axis=1)
            m = jnp.max(s, axis=-1, keepdims=True)
            p = jnp.exp2((s - m) * (scale * LOG2_E))
            l = jnp.sum(p, axis=-1, keepdims=True)
            o = jnp.dot(p.astype(BF16), vv.astype(BF16), preferred_element_type=F32) / l
            lse = jnp.broadcast_to(m * scale + jnp.log(l), (blk, blk))
            finish(n, cur, run, o, lse)

        def body(it, carry):
            for u in range(unroll):
                one_block(it * unroll + u)
            return carry

        assert n_blocks % unroll == 0
        lax.fori_loop(0, n_blocks // unroll, body, 0)

    def al(x, mult):
        return pl.multiple_of(x, mult)

    def divmod_pow2(x, d):
        shift = d.bit_length() - 1
        assert d == 1 << shift
        return x >> shift, x & (d - 1)

    def keep(o_scr, lse_scr):
        def finish(n, cur, run, o, lse):
            scatter(o_scr, cur, run, o)
            scatter(lse_scr, cur, run, lse)
        return finish

    def starts_d1(n):
        def runs(b):
            span, sub = divmod_pow2(b, 16)
            return [al(span * PERM_SPAN + r * PERM_RUN + sub * 8, 8) for r in range(PERM_RES)]
        return runs(n), runs(jnp.maximum(n - 1, 0)), n > 0

    branch(q1_ref, k1_ref, v1_ref, spans * 16, starts_d1, 8,
           lambda i: 16 * (i & 7) + (i >> 3), keep(o1_scr, lse1_scr))

    def starts_d4(n):
        res, c = divmod_pow2(n, spans * 4)
        def runs(cc):
            span, sub = divmod_pow2(cc, 4)
            return [al(span * PERM_SPAN + (res + 4 * u) * PERM_RUN + sub * 32, 32) for u in range(4)]
        return runs(c), runs(jnp.maximum(c - 1, 0)), c > 0

    branch(q4_ref, k4_ref, v4_ref, spans * 16, starts_d4, 32,
           lambda i: 4 * (i & 31) + (i >> 5), keep(o4_scr, lse4_scr))

    def starts_d16(n):
        res, span = divmod_pow2(n, spans)
        def runs(sp):
            return [al(sp * PERM_SPAN + res * PERM_RUN, PERM_RUN)]
        return runs(span), runs(jnp.maximum(span - 1, 0)), span > 0

    def combine(n, cur, run, o16, lse16):
        lse1 = gather(lse1_scr, cur, run)
        lse4 = gather(lse4_scr, cur, run)
        top = jnp.maximum(jnp.maximum(lse1, lse4), lse16)
        e1 = jnp.exp(lse1 - top)
        e4 = jnp.exp(lse4 - top)
        e16 = jnp.exp(lse16 - top)
        den = e1 + e4 + e16
        o = (e1 / den) * gather(o1_scr, cur, run) + (e4 / den) * gather(o4_scr, cur, run)
        o = o + (e16 / den) * o16
        res, span = divmod_pow2(n, spans)
        otok_scr[pl.ds(span * PERM_SPAN + res, PERM_RUN, stride=PERM_RES), :] = o

    branch(q16_ref, k16_ref, v16_ref, spans * 16, starts_d16, 128, lambda i: i, combine)
    o_ref[...] = otok_scr[...].astype(o_ref.dtype)


def _dilated_attention(q, kv, *, bsz, s_len, heads, unroll=16):
    t_len = q.shape[0]
    hd = DIL_HD
    assert s_len % PERM_SPAN == 0
    spans = s_len // PERM_SPAN
    assert q.shape[1] == 3 * heads * hd and kv.shape[1] == 6 * heads * hd

    def qspec(g):
        return pl.BlockSpec((s_len, hd), lambda b, h: (b, g * heads + h))

    def kvspec(g, which):
        return pl.BlockSpec((s_len, hd), lambda b, h: (b, (2 * g + which) * heads + h))

    return pl.pallas_call(
        functools.partial(_dilated_kernel, spans=spans, unroll=unroll),
        grid=(bsz, heads),
        in_specs=[qspec(0), qspec(1), qspec(2),
                  kvspec(0, 0), kvspec(1, 0), kvspec(2, 0),
                  kvspec(0, 1), kvspec(1, 1), kvspec(2, 1)],
        out_specs=pl.BlockSpec((s_len, hd), lambda b, h: (b, h)),
        out_shape=jax.ShapeDtypeStruct((t_len, heads * hd), BF16),
        scratch_shapes=[pltpu.VMEM((s_len, hd), F32)] * 5,
        compiler_params=_cparams(2),
        name="dilated_attention",
    )(q, q, q, kv, kv, kv, kv, kv, kv)


def _to_perm(x, bsz, s_len):
    d = x.shape[-1]
    x = x.reshape(bsz, s_len // PERM_SPAN, PERM_RUN, PERM_RES, d)
    return x.transpose(0, 1, 3, 2, 4).reshape(bsz * s_len, d)


def _channel_mix(z, w1, w2, ln_g, ln_b, layer, *, with_bf16):
    hb = _ln_cast(z, ln_g[layer, 0], ln_b[layer, 0], name=f"ln_mix_{layer}")
    hmid = _matmul(hb, w1, layer, out_dtype=BF16, act="relu2", name=f"mlp_up_{layer}")
    mix = _matmul_ktiled(hmid, w2, layer, name=f"mlp_down_{layer}")
    return _ln_res_ln(z, mix, ln_g[layer, 0], ln_b[layer, 0], ln_g[layer, 1], ln_b[layer, 1],
                      with_bf16=with_bf16, name=f"ln_mlp_{layer}")


def kernel(x, a_w_in, a_w_g1, a_w_g2, a_b_g, a_gn_g, a_gn_b, a_w_out, b_w_q, kv_w, b_w_out,
           mlp_w1, mlp_w2, ln_g, ln_b):
    bsz, s_len, d = x.shape
    t_len = bsz * s_len
    h = x.reshape(t_len, d)

    g, hb = _gate(h, a_w_g1[0], a_w_g2[0], a_b_g[0].reshape(1, -1))
    proj = _matmul(hb, a_w_in, 0, out_dtype=F32, name="gla_in_proj")
    y = _gla(proj, g, a_gn_g[0].reshape(1, -1), a_gn_b[0].reshape(1, -1),
             bsz=bsz, s_len=s_len, heads=GLA_HEADS)
    z = _matmul(y, a_w_out, 0, out_dtype=F32, residual=h, res_scale=DEEPNORM_ALPHA,
                tm=1024, x_buffers=2, name="gla_out_proj")
    h, hb = _channel_mix(z, mlp_w1, mlp_w2, ln_g, ln_b, 0, with_bf16=True)

    hb_perm = _to_perm(hb, bsz, s_len)
    q = _matmul(hb_perm, b_w_q, 0, out_dtype=F32, name="dil_q_proj")
    kv = _matmul(hb_perm, kv_w, 0, out_dtype=F32, name="dil_kv_proj")
    o = _dilated_attention(q, kv, bsz=bsz, s_len=s_len, heads=DIL_HEADS)
    z = _matmul(o, b_w_out, 0, out_dtype=F32, residual=h, res_scale=DEEPNORM_ALPHA,
                tm=1024, x_buffers=2, name="dil_out_proj")
    (h,) = _channel_mix(z, mlp_w1, mlp_w2, ln_g, ln_b, 1, with_bf16=False)
    return h.reshape(bsz, s_len, d)
```

```python
import functools

import jax
import jax.numpy as jnp
from jax import lax
from jax.experimental import pallas as pl
from jax.experimental.pallas import tpu as pltpu

F32 = jnp.float32
BF16 = jnp.bfloat16

GLA_HEADS = 8
GLA_RANK = 16
GLA_TAU = 16.0
GLA_CHUNK = 64
DIL_HEADS = 32
DIL_HD = 128
LN_EPS = 1e-5
DEPTH = 2
DEEPNORM_ALPHA = (2 * DEPTH) ** 0.25
LOG2_E = 1.4426950408889634

V7X_VMEM_BYTES = 64 * 1024 * 1024
VMEM_LIMIT = V7X_VMEM_BYTES - 8 * 1024 * 1024

PERM_RES = 16
PERM_RUN = 128
PERM_SPAN = PERM_RES * PERM_RUN


def _cparams(n_axes):
    return pltpu.CompilerParams(
        dimension_semantics=("arbitrary",) * n_axes,
        vmem_limit_bytes=VMEM_LIMIT,
    )


def _mm_kernel(x_ref, w_ref, *rest, act, res_scale):
    o_ref = rest[-1]
    w = w_ref[...].astype(BF16)
    acc = jnp.dot(x_ref[...], w, preferred_element_type=F32)
    if act == "relu2":
        acc = jnp.square(jnp.maximum(acc, 0.0))
    if res_scale is not None:
        acc = res_scale * rest[0][...] + acc
    o_ref[...] = acc.astype(o_ref.dtype)


def _matmul(x, w, layer, *, out_dtype, act=None, residual=None, res_scale=None,
            tm=2048, tn=512, x_buffers=1, name):
    m, k = x.shape
    n = w.shape[-1]
    assert m % tm == 0 and n % tn == 0 and w.shape[-2] == k
    assert (residual is None) == (res_scale is None)
    if w.ndim == 3:
        w_spec = pl.BlockSpec((None, k, tn), lambda i, j: (layer, 0, j))
    else:
        w_spec = pl.BlockSpec((k, tn), lambda i, j: (0, j))
    x_spec = pl.BlockSpec((tm, k), lambda i, j: (i, 0), pipeline_mode=pl.Buffered(x_buffers))
    tile_spec = pl.BlockSpec((tm, tn), lambda i, j: (i, j))
    operands, in_specs = [x, w], [x_spec, w_spec]
    if residual is not None:
        operands.append(residual)
        in_specs.append(tile_spec)
    return pl.pallas_call(
        functools.partial(_mm_kernel, act=act, res_scale=res_scale),
        grid=(m // tm, n // tn),
        in_specs=in_specs,
        out_specs=tile_spec,
        out_shape=jax.ShapeDtypeStruct((m, n), out_dtype),
        compiler_params=_cparams(2),
        name=name,
    )(*operands)


def _mm_acc_kernel(x_ref, w_ref, o_ref, *, n_chunk):
    @pl.when(pl.program_id(2) == 0)
    def _():
        o_ref[...] = jnp.zeros_like(o_ref)

    for n0 in range(0, o_ref.shape[1], n_chunk):
        w = w_ref[:, n0:n0 + n_chunk].astype(BF16)
        o_ref[:, n0:n0 + n_chunk] += jnp.dot(x_ref[...], w, preferred_element_type=F32)


def _matmul_ktiled(x, w, layer, *, tm=2048, tn=1024, tk=2048, n_chunk=256, name):
    m, k = x.shape
    n = w.shape[-1]
    assert m % tm == 0 and n % tn == 0 and k % tk == 0 and tn % n_chunk == 0
    return pl.pallas_call(
        functools.partial(_mm_acc_kernel, n_chunk=n_chunk),
        grid=(n // tn, m // tm, k // tk),
        in_specs=[
            pl.BlockSpec((tm, tk), lambda j, i, kk: (i, kk)),
            pl.BlockSpec((None, tk, tn), lambda j, i, kk: (layer, kk, j)),
        ],
        out_specs=pl.BlockSpec((tm, tn), lambda j, i, kk: (i, j)),
        out_shape=jax.ShapeDtypeStruct((m, n), F32),
        compiler_params=_cparams(3),
        name=name,
    )(x, w)


def _gate_kernel(x_ref, w1_ref, w2_ref, b_ref, o_ref, xb_ref):
    xb = x_ref[...].astype(BF16)
    xb_ref[...] = xb
    t = jnp.dot(xb, w1_ref[...].astype(BF16), preferred_element_type=F32)
    z = jnp.dot(t.astype(BF16), w2_ref[...].astype(BF16), preferred_element_type=F32)
    z = z + b_ref[...]
    log_sig = jnp.minimum(z, 0.0) - jnp.log1p(jnp.exp(-jnp.abs(z)))
    o_ref[...] = log_sig / GLA_TAU


def _gate(x, w1, w2, b, *, tm=512):
    m, k = x.shape
    rank = w1.shape[-1]
    n = w2.shape[-1]
    return pl.pallas_call(
        _gate_kernel,
        grid=(m // tm,),
        in_specs=[
            pl.BlockSpec((tm, k), lambda i: (i, 0)),
            pl.BlockSpec((k, rank), lambda i: (0, 0)),
            pl.BlockSpec((rank, n), lambda i: (0, 0)),
            pl.BlockSpec((1, n), lambda i: (0, 0)),
        ],
        out_specs=[pl.BlockSpec((tm, n), lambda i: (i, 0)),
                   pl.BlockSpec((tm, k), lambda i: (i, 0))],
        out_shape=[jax.ShapeDtypeStruct((m, n), F32), jax.ShapeDtypeStruct((m, k), BF16)],
        compiler_params=_cparams(1),
        name="gla_gate",
    )(x, w1, w2, b)


def _chunk_cumsums(tri_bf16, g, c_len):
    dk = g.shape[1]
    g_hi = g.astype(BF16)
    r1 = g - g_hi.astype(F32)
    g_mid = r1.astype(BF16)
    g_lo = (r1 - g_mid.astype(F32)).astype(BF16)
    n_chunks = g.shape[0] // c_len
    side = jnp.concatenate(
        [part[c * c_len:(c + 1) * c_len, :] for c in range(n_chunks) for part in (g_hi, g_mid, g_lo)],
        axis=1)
    sums = jnp.dot(tri_bf16, side, preferred_element_type=F32)
    out = []
    for c in range(n_chunks):
        hi, mid, lo = (sums[:, (3 * c + t) * dk:(3 * c + t + 1) * dk] for t in range(3))
        out.append((hi + mid) + lo)
    return out


def _gla_kernel(q_ref, k_ref, v_ref, r_ref, g_ref, gng_ref, gnb_ref, y_ref,
                state_ref, o_scr, *, dk, chunks):
    nseq = q_ref.shape[0]

    @pl.when(pl.program_id(1) == 0)
    def _():
        state_ref[...] = jnp.zeros_like(state_ref)

    c_len = GLA_CHUNK
    row = lax.broadcasted_iota(jnp.int32, (c_len, c_len), 0)
    col = lax.broadcasted_iota(jnp.int32, (c_len, c_len), 1)
    causal = row >= col
    tri = causal.astype(BF16)
    nt = (((1,), (1,)), ((), ()))
    tn = (((0,), (0,)), ((), ()))

    bcums = [_chunk_cumsums(tri, g_ref[s], c_len) for s in range(nseq)]

    for c in range(chunks):
        sl = slice(c * c_len, (c + 1) * c_len)
        for s in range(nseq):
            bcum = bcums[s][c]
            b_mid = bcum[c_len // 2 - 1:c_len // 2, :]
            b_last = bcum[c_len - 1:c_len, :]
            qc = q_ref[s, sl, :] * (dk ** -0.5)
            kc = k_ref[s, sl, :]
            vc = v_ref[s, sl, :].astype(BF16)
            qa = (qc * jnp.exp(bcum - b_mid)).astype(BF16)
            ka = (kc * jnp.exp(b_mid - bcum)).astype(BF16)
            attn = lax.dot_general(qa, ka, nt, preferred_element_type=F32)
            attn = jnp.where(causal, attn, 0.0)
            o_intra = jnp.dot(attn.astype(BF16), vc, preferred_element_type=F32)
            qs = (qc * jnp.exp(bcum)).astype(BF16)
            ks = (kc * jnp.exp(b_last - bcum)).astype(BF16)
            st = state_ref[s]
            o_inter = lax.dot_general(qs, st.astype(BF16), nt, preferred_element_type=F32)
            upd = lax.dot_general(vc, ks, tn, preferred_element_type=F32)
            state_ref[s] = jnp.exp(b_last) * st + upd
            o_scr[s, sl, :] = o_intra + o_inter

    for s in range(nseq):
        o = o_scr[s]
        mu = jnp.mean(o, axis=-1, keepdims=True)
        var = jnp.mean(jnp.square(o - mu), axis=-1, keepdims=True)
        o = (o - mu) * lax.rsqrt(var + LN_EPS)
        o = o * gng_ref[...] + gnb_ref[...]
        r = r_ref[s]
        silu = r * (1.0 / (1.0 + jnp.exp(-r)))
        y_ref[s] = (silu * o).astype(y_ref.dtype)


def _gla(proj, g, gn_g, gn_b, *, bsz, s_len, heads, tb=1024):
    t_len = proj.shape[0]
    dk = g.shape[1] // heads
    dv = gn_g.shape[1] // heads
    assert proj.shape[1] == 2 * heads * dk + 2 * heads * dv and dv == 2 * dk
    assert s_len % tb == 0 and tb % GLA_CHUNK == 0
    proj3 = proj.reshape(bsz, s_len, proj.shape[1])
    g3 = g.reshape(bsz, s_len, g.shape[1])
    kernel = functools.partial(_gla_kernel, dk=dk, chunks=tb // GLA_CHUNK)
    y = pl.pallas_call(
        kernel,
        grid=(heads, s_len // tb),
        in_specs=[
            pl.BlockSpec((bsz, tb, dk), lambda h, i: (0, i, h)),
            pl.BlockSpec((bsz, tb, dk), lambda h, i: (0, i, heads + h)),
            pl.BlockSpec((bsz, tb, dv), lambda h, i: (0, i, heads + h)),
            pl.BlockSpec((bsz, tb, dv), lambda h, i: (0, i, 2 * heads + h)),
            pl.BlockSpec((bsz, tb, dk), lambda h, i: (0, i, h)),
            pl.BlockSpec((1, dv), lambda h, i: (0, h)),
            pl.BlockSpec((1, dv), lambda h, i: (0, h)),
        ],
        out_specs=pl.BlockSpec((bsz, tb, dv), lambda h, i: (0, i, h)),
        out_shape=jax.ShapeDtypeStruct((bsz, s_len, heads * dv), BF16),
        scratch_shapes=[pltpu.VMEM((bsz, dv, dk), F32), pltpu.VMEM((bsz, tb, dv), F32)],
        compiler_params=_cparams(2),
        name="gla_mixer",
    )(proj3, proj3, proj3, proj3, g3, gn_g, gn_b)
    return y.reshape(t_len, heads * dv)


def _layer_norm_rows(z, g_ref, b_ref):
    mu = jnp.mean(z, axis=-1, keepdims=True)
    var = jnp.mean(jnp.square(z - mu), axis=-1, keepdims=True)
    return (z - mu) * lax.rsqrt(var + LN_EPS) * g_ref[...] + b_ref[...]


def _ln_cast_kernel(z_ref, g_ref, b_ref, ob_ref):
    ob_ref[...] = _layer_norm_rows(z_ref[...], g_ref, b_ref).astype(ob_ref.dtype)


def _ln_res_ln_kernel(z_ref, mix_ref, g1_ref, b1_ref, g2_ref, b2_ref, *out_refs):
    h = _layer_norm_rows(z_ref[...], g1_ref, b1_ref)
    y = _layer_norm_rows(DEEPNORM_ALPHA * h + mix_ref[...], g2_ref, b2_ref)
    for o_ref in out_refs:
        o_ref[...] = y.astype(o_ref.dtype)


def _ln_specs(d, tm):
    return pl.BlockSpec((tm, d), lambda i: (i, 0)), pl.BlockSpec((1, d), lambda i: (0, 0))


def _ln_cast(z, gain, bias, *, tm=256, name):
    m, d = z.shape
    row_spec, vec_spec = _ln_specs(d, tm)
    return pl.pallas_call(
        _ln_cast_kernel,
        grid=(m // tm,),
        in_specs=[row_spec, vec_spec, vec_spec],
        out_specs=row_spec,
        out_shape=jax.ShapeDtypeStruct((m, d), BF16),
        compiler_params=_cparams(1),
        name=name,
    )(z, gain.reshape(1, d), bias.reshape(1, d))


def _ln_res_ln(z, mix, gain1, bias1, gain2, bias2, *, with_bf16, tm=256, name):
    m, d = z.shape
    row_spec, vec_spec = _ln_specs(d, tm)
    out_shape = [jax.ShapeDtypeStruct((m, d), F32)]
    if with_bf16:
        out_shape.append(jax.ShapeDtypeStruct((m, d), BF16))
    vecs = [v.reshape(1, d) for v in (gain1, bias1, gain2, bias2)]
    return pl.pallas_call(
        _ln_res_ln_kernel,
        grid=(m // tm,),
        in_specs=[row_spec, row_spec] + [vec_spec] * 4,
        out_specs=[row_spec] * len(out_shape),
        out_shape=out_shape,
        compiler_params=_cparams(1),
        name=name,
    )(z, mix, *vecs)


def _dilated_kernel(q1_ref, q4_ref, q16_ref, k1_ref, k4_ref, k16_ref,
                    v1_ref, v4_ref, v16_ref, o_ref, o1_scr, lse1_scr, o4_scr, lse4_scr,
                    otok_scr, *, spans, unroll):
    blk = 128
    scale = DIL_HD ** -0.5
    nt = (((1,), (1,)), ((), ()))
    ridx = lax.broadcasted_iota(jnp.int32, (blk, blk), 0)
    cidx = lax.broadcasted_iota(jnp.int32, (blk, blk), 1)

    def gather(ref, starts, run):
        if len(starts) == 1:
            return ref[pl.ds(starts[0], run), :]
        return jnp.concatenate([ref[pl.ds(s, run), :] for s in starts], axis=0)

    def scatter(ref, starts, run, val):
        for n, s in enumerate(starts):
            ref[pl.ds(s, run), :] = val[n * run:(n + 1) * run, :]

    def branch(q_ref, k_ref, v_ref, n_blocks, starts_fn, run, pos_fn, finish):
        pos_q = pos_fn(ridx)
        pos_k = pos_fn(cidx)
        mask = jnp.concatenate([pos_k >= pos_q, pos_k <= pos_q], axis=1)

        def one_block(n):
            cur, prev, has_prev = starts_fn(n)
            q = gather(q_ref, cur, run).astype(BF16)
            kk = jnp.concatenate([gather(k_ref, prev, run), gather(k_ref, cur, run)], axis=0)
            vv = jnp.concatenate([gather(v_ref, prev, run), gather(v_ref, cur, run)], axis=0)
            s = lax.dot_general(q, kk.astype(BF16), nt, preferred_element_type=F32)
            s = jnp.where(mask, s, -jnp.inf)
            no_prev = jnp.where(has_prev, 0.0, -jnp.inf)
            s = jnp.concatenate([s[:, :blk] + no_prev, s[:, blk:]], axis=1)
            m = jnp.max(s, axis=-1, keepdims=True)
            p = jnp.exp2((s - m) * (scale * LOG2_E))
            l = jnp.sum(p, axis=-1, keepdims=True)
            o = jnp.dot(p.astype(BF16), vv.astype(BF16), preferred_element_type=F32) / l
            lse = jnp.broadcast_to(m * scale + jnp.log(l), (blk, blk))
            finish(n, cur, run, o, lse)

        def body(it, carry):
            for u in range(unroll):
                one_block(it * unroll + u)
            return carry

        assert n_blocks % unroll == 0
        lax.fori_loop(0, n_blocks // unroll, body, 0)

    def al(x, mult):
        return pl.multiple_of(x, mult)

    def divmod_pow2(x, d):
        shift = d.bit_length() - 1
        assert d == 1 << shift
        return x >> shift, x & (d - 1)

    def keep(o_scr, lse_scr):
        def finish(n, cur, run, o, lse):
            scatter(o_scr, cur, run, o)
            scatter(lse_scr, cur, run, lse)
        return finish

    def starts_d1(n):
        def runs(b):
            span, sub = divmod_pow2(b, 16)
            return [al(span * PERM_SPAN + r * PERM_RUN + sub * 8, 8) for r in range(PERM_RES)]
        return runs(n), runs(jnp.maximum(n - 1, 0)), n > 0

    branch(q1_ref, k1_ref, v1_ref, spans * 16, starts_d1, 8,
           lambda i: 16 * (i & 7) + (i >> 3), keep(o1_scr, lse1_scr))

    def starts_d4(n):
        res, c = divmod_pow2(n, spans * 4)
        def runs(cc):
            span, sub = divmod_pow2(cc, 4)
            return [al(span * PERM_SPAN + (res + 4 * u) * PERM_RUN + sub * 32, 32) for u in range(4)]
        return runs(c), runs(jnp.maximum(c - 1, 0)), c > 0

    branch(q4_ref, k4_ref, v4_ref, spans * 16, starts_d4, 32,
           lambda i: 4 * (i & 31) + (i >> 5), keep(o4_scr, lse4_scr))

    def starts_d16(n):
        res, span = divmod_pow2(n, spans)
        def runs(sp):
            return [al(sp * PERM_SPAN + res * PERM_RUN, PERM_RUN)]
        return runs(span), runs(jnp.maximum(span - 1, 0)), span > 0

    def combine(n, cur, run, o16, lse16):
        lse1 = gather(lse1_scr, cur, run)
        lse4 = gather(lse4_scr, cur, run)
        top = jnp.maximum(jnp.maximum(lse1, lse4), lse16)
        e1 = jnp.exp(lse1 - top)
        e4 = jnp.exp(lse4 - top)
        e16 = jnp.exp(lse16 - top)
        den = e1 + e4 + e16
        o = (e1 / den) * gather(o1_scr, cur, run) + (e4 / den) * gather(o4_scr, cur, run)
        o = o + (e16 / den) * o16
        res, span = divmod_pow2(n, spans)
        otok_scr[pl.ds(span * PERM_SPAN + res, PERM_RUN, stride=PERM_RES), :] = o

    branch(q16_ref, k16_ref, v16_ref, spans * 16, starts_d16, 128, lambda i: i, combine)
    o_ref[...] = otok_scr[...].astype(o_ref.dtype)


def _dilated_attention(q, kv, *, bsz, s_len, heads, unroll=32):
    t_len = q.shape[0]
    hd = DIL_HD
    assert s_len % PERM_SPAN == 0
    spans = s_len // PERM_SPAN
    assert q.shape[1] == 3 * heads * hd and kv.shape[1] == 6 * heads * hd

    def qspec(g):
        return pl.BlockSpec((s_len, hd), lambda b, h: (b, g * heads + h))

    def kvspec(g, which):
        return pl.BlockSpec((s_len, hd), lambda b, h: (b, (2 * g + which) * heads + h))

    return pl.pallas_call(
        functools.partial(_dilated_kernel, spans=spans, unroll=unroll),
        grid=(bsz, heads),
        in_specs=[qspec(0), qspec(1), qspec(2),
                  kvspec(0, 0), kvspec(1, 0), kvspec(2, 0),
                  kvspec(0, 1), kvspec(1, 1), kvspec(2, 1)],
        out_specs=pl.BlockSpec((s_len, hd), lambda b, h: (b, h)),
        out_shape=jax.ShapeDtypeStruct((t_len, heads * hd), BF16),
        scratch_shapes=[pltpu.VMEM((s_len, hd), F32)] * 5,
        compiler_params=_cparams(2),
        name="dilated_attention",
    )(q, q, q, kv, kv, kv, kv, kv, kv)


def _to_perm(x, bsz, s_len):
    d = x.shape[-1]
    x = x.reshape(bsz, s_len // PERM_SPAN, PERM_RUN, PERM_RES, d)
    return x.transpose(0, 1, 3, 2, 4).reshape(bsz * s_len, d)


def _channel_mix(z, w1, w2, ln_g, ln_b, layer, *, with_bf16):
    hb = _ln_cast(z, ln_g[layer, 0], ln_b[layer, 0], name=f"ln_mix_{layer}")
    hmid = _matmul(hb, w1, layer, out_dtype=BF16, act="relu2", name=f"mlp_up_{layer}")
    mix = _matmul_ktiled(hmid, w2, layer, name=f"mlp_down_{layer}")
    return _ln_res_ln(z, mix, ln_g[layer, 0], ln_b[layer, 0], ln_g[layer, 1], ln_b[layer, 1],
                      with_bf16=with_bf16, name=f"ln_mlp_{layer}")


def kernel(x, a_w_in, a_w_g1, a_w_g2, a_b_g, a_gn_g, a_gn_b, a_w_out, b_w_q, kv_w, b_w_out,
           mlp_w1, mlp_w2, ln_g, ln_b):
    bsz, s_len, d = x.shape
    t_len = bsz * s_len
    h = x.reshape(t_len, d)

    g, hb = _gate(h, a_w_g1[0], a_w_g2[0], a_b_g[0].reshape(1, -1))
    proj = _matmul(hb, a_w_in, 0, out_dtype=F32, name="gla_in_proj")
    y = _gla(proj, g, a_gn_g[0].reshape(1, -1), a_gn_b[0].reshape(1, -1),
             bsz=bsz, s_len=s_len, heads=GLA_HEADS)
    z = _matmul(y, a_w_out, 0, out_dtype=F32, residual=h, res_scale=DEEPNORM_ALPHA,
                tm=1024, x_buffers=2, name="gla_out_proj")
    h, hb = _channel_mix(z, mlp_w1, mlp_w2, ln_g, ln_b, 0, with_bf16=True)

    hb_perm = _to_perm(hb, bsz, s_len)
    q = _matmul(hb_perm, b_w_q, 0, out_dtype=F32, name="dil_q_proj")
    kv = _matmul(hb_perm, kv_w, 0, out_dtype=F32, name="dil_kv_proj")
    o = _dilated_attention(q, kv, bsz=bsz, s_len=s_len, heads=DIL_HEADS)
    z = _matmul(o, b_w_out, 0, out_dtype=F32, residual=h, res_scale=DEEPNORM_ALPHA,
                tm=1024, x_buffers=2, name="dil_out_proj")
    (h,) = _channel_mix(z, mlp_w1, mlp_w2, ln_g, ln_b, 1, with_bf16=False)
    return h.reshape(bsz, s_len, d)
```

```python
import functools

import jax
import jax.numpy as jnp
from jax import lax
from jax.experimental import pallas as pl
from jax.experimental.pallas import tpu as pltpu

F32 = jnp.float32
BF16 = jnp.bfloat16

GLA_HEADS = 8
GLA_RANK = 16
GLA_TAU = 16.0
GLA_CHUNK = 64
DIL_HEADS = 32
DIL_HD = 128
LN_EPS = 1e-5
DEPTH = 2
DEEPNORM_ALPHA = (2 * DEPTH) ** 0.25
LOG2_E = 1.4426950408889634

V7X_VMEM_BYTES = 64 * 1024 * 1024
VMEM_LIMIT = V7X_VMEM_BYTES - 8 * 1024 * 1024

PERM_RES = 16
PERM_RUN = 128
PERM_SPAN = PERM_RES * PERM_RUN


def _cparams(n_axes):
    return pltpu.CompilerParams(
        dimension_semantics=("arbitrary",) * n_axes,
        vmem_limit_bytes=VMEM_LIMIT,
    )


def _mm_kernel(x_ref, w_ref, *rest, act, res_scale):
    o_ref = rest[-1]
    w = w_ref[...].astype(BF16)
    acc = jnp.dot(x_ref[...], w, preferred_element_type=F32)
    if act == "relu2":
        acc = jnp.square(jnp.maximum(acc, 0.0))
    if res_scale is not None:
        acc = res_scale * rest[0][...] + acc
    o_ref[...] = acc.astype(o_ref.dtype)


def _matmul(x, w, layer, *, out_dtype, act=None, residual=None, res_scale=None,
            tm=2048, tn=512, x_buffers=1, name):
    m, k = x.shape
    n = w.shape[-1]
    assert m % tm == 0 and n % tn == 0 and w.shape[-2] == k
    assert (residual is None) == (res_scale is None)
    if w.ndim == 3:
        w_spec = pl.BlockSpec((None, k, tn), lambda i, j: (layer, 0, j))
    else:
        w_spec = pl.BlockSpec((k, tn), lambda i, j: (0, j))
    x_spec = pl.BlockSpec((tm, k), lambda i, j: (i, 0), pipeline_mode=pl.Buffered(x_buffers))
    tile_spec = pl.BlockSpec((tm, tn), lambda i, j: (i, j))
    operands, in_specs = [x, w], [x_spec, w_spec]
    if residual is not None:
        operands.append(residual)
        in_specs.append(tile_spec)
    return pl.pallas_call(
        functools.partial(_mm_kernel, act=act, res_scale=res_scale),
        grid=(m // tm, n // tn),
        in_specs=in_specs,
        out_specs=tile_spec,
        out_shape=jax.ShapeDtypeStruct((m, n), out_dtype),
        compiler_params=_cparams(2),
        name=name,
    )(*operands)


def _mm_acc_kernel(x_ref, w_ref, o_ref, *, n_chunk):
    @pl.when(pl.program_id(2) == 0)
    def _():
        o_ref[...] = jnp.zeros_like(o_ref)

    for n0 in range(0, o_ref.shape[1], n_chunk):
        w = w_ref[:, n0:n0 + n_chunk].astype(BF16)
        o_ref[:, n0:n0 + n_chunk] += jnp.dot(x_ref[...], w, preferred_element_type=F32)


def _matmul_ktiled(x, w, layer, *, tm=2048, tn=1024, tk=2048, n_chunk=256, name):
    m, k = x.shape
    n = w.shape[-1]
    assert m % tm == 0 and n % tn == 0 and k % tk == 0 and tn % n_chunk == 0
    return pl.pallas_call(
        functools.partial(_mm_acc_kernel, n_chunk=n_chunk),
        grid=(n // tn, m // tm, k // tk),
        in_specs=[
            pl.BlockSpec((tm, tk), lambda j, i, kk: (i, kk)),
            pl.BlockSpec((None, tk, tn), lambda j, i, kk: (layer, kk, j)),
        ],
        out_specs=pl.BlockSpec((tm, tn), lambda j, i, kk: (i, j)),
        out_shape=jax.ShapeDtypeStruct((m, n), F32),
        compiler_params=_cparams(3),
        name=name,
    )(x, w)


def _gate_kernel(x_ref, w1_ref, w2_ref, b_ref, o_ref, xb_ref):
    xb = x_ref[...].astype(BF16)
    xb_ref[...] = xb
    t = jnp.dot(xb, w1_ref[...].astype(BF16), preferred_element_type=F32)
    z = jnp.dot(t.astype(BF16), w2_ref[...].astype(BF16), preferred_element_type=F32)
    z = z + b_ref[...]
    log_sig = jnp.minimum(z, 0.0) - jnp.log1p(jnp.exp(-jnp.abs(z)))
    o_ref[...] = log_sig / GLA_TAU


def _gate(x, w1, w2, b, *, tm=512):
    m, k = x.shape
    rank = w1.shape[-1]
    n = w2.shape[-1]
    return pl.pallas_call(
        _gate_kernel,
        grid=(m // tm,),
        in_specs=[
            pl.BlockSpec((tm, k), lambda i: (i, 0)),
            pl.BlockSpec((k, rank), lambda i: (0, 0)),
            pl.BlockSpec((rank, n), lambda i: (0, 0)),
            pl.BlockSpec((1, n), lambda i: (0, 0)),
        ],
        out_specs=[pl.BlockSpec((tm, n), lambda i: (i, 0)),
                   pl.BlockSpec((tm, k), lambda i: (i, 0))],
        out_shape=[jax.ShapeDtypeStruct((m, n), F32), jax.ShapeDtypeStruct((m, k), BF16)],
        compiler_params=_cparams(1),
        name="gla_gate",
    )(x, w1, w2, b)


def _chunk_cumsums(tri_bf16, g, c_len):
    dk = g.shape[1]
    g_hi = g.astype(BF16)
    r1 = g - g_hi.astype(F32)
    g_mid = r1.astype(BF16)
    g_lo = (r1 - g_mid.astype(F32)).astype(BF16)
    n_chunks = g.shape[0] // c_len
    side = jnp.concatenate(
        [part[c * c_len:(c + 1) * c_len, :] for c in range(n_chunks) for part in (g_hi, g_mid, g_lo)],
        axis=1)
    sums = jnp.dot(tri_bf16, side, preferred_element_type=F32)
    out = []
    for c in range(n_chunks):
        hi, mid, lo = (sums[:, (3 * c + t) * dk:(3 * c + t + 1) * dk] for t in range(3))
        out.append((hi + mid) + lo)
    return out


def _gla_kernel(q_ref, k_ref, v_ref, r_ref, g_ref, gng_ref, gnb_ref, y_ref,
                state_ref, o_scr, *, dk, chunks):
    nseq = q_ref.shape[0]

    @pl.when(pl.program_id(1) == 0)
    def _():
        state_ref[...] = jnp.zeros_like(state_ref)

    c_len = GLA_CHUNK
    row = lax.broadcasted_iota(jnp.int32, (c_len, c_len), 0)
    col = lax.broadcasted_iota(jnp.int32, (c_len, c_len), 1)
    causal = row >= col
    tri = causal.astype(BF16)
    nt = (((1,), (1,)), ((), ()))
    tn = (((0,), (0,)), ((), ()))

    bcums = [_chunk_cumsums(tri, g_ref[s], c_len) for s in range(nseq)]

    for c in range(chunks):
        sl = slice(c * c_len, (c + 1) * c_len)
        for s in range(nseq):
            bcum = bcums[s][c]
            b_mid = bcum[c_len // 2 - 1:c_len // 2, :]
            b_last = bcum[c_len - 1:c_len, :]
            qc = q_ref[s, sl, :] * (dk ** -0.5)
            kc = k_ref[s, sl, :]
            vc = v_ref[s, sl, :].astype(BF16)
            qa = (qc * jnp.exp(bcum - b_mid)).astype(BF16)
            ka = (kc * jnp.exp(b_mid - bcum)).astype(BF16)
            attn = lax.dot_general(qa, ka, nt, preferred_element_type=F32)
            attn = jnp.where(causal, attn, 0.0)
            o_intra = jnp.dot(attn.astype(BF16), vc, preferred_element_type=F32)
            qs = (qc * jnp.exp(bcum)).astype(BF16)
            ks = (kc * jnp.exp(b_last - bcum)).astype(BF16)
            st = state_ref[s]
            o_inter = lax.dot_general(qs, st.astype(BF16), nt, preferred_element_type=F32)
            upd = lax.dot_general(vc, ks, tn, preferred_element_type=F32)
            state_ref[s] = jnp.exp(b_last) * st + upd
            o_scr[s, sl, :] = o_intra + o_inter

    for s in range(nseq):
        o = o_scr[s]
        mu = jnp.mean(o, axis=-1, keepdims=True)
        var = jnp.mean(jnp.square(o - mu), axis=-1, keepdims=True)
        o = (o - mu) * lax.rsqrt(var + LN_EPS)
        o = o * gng_ref[...] + gnb_ref[...]
        r = r_ref[s]
        silu = r * (1.0 / (1.0 + jnp.exp(-r)))
        y_ref[s] = (silu * o).astype(y_ref.dtype)


def _gla(proj, g, gn_g, gn_b, *, bsz, s_len, heads, tb=1024):
    t_len = proj.shape[0]
    dk = g.shape[1] // heads
    dv = gn_g.shape[1] // heads
    assert proj.shape[1] == 2 * heads * dk + 2 * heads * dv and dv == 2 * dk
    assert s_len % tb == 0 and tb % GLA_CHUNK == 0
    proj3 = proj.reshape(bsz, s_len, proj.shape[1])
    g3 = g.reshape(bsz, s_len, g.shape[1])
    kernel = functools.partial(_gla_kernel, dk=dk, chunks=tb // GLA_CHUNK)
    y = pl.pallas_call(
        kernel,
        grid=(heads, s_len // tb),
        in_specs=[
            pl.BlockSpec((bsz, tb, dk), lambda h, i: (0, i, h)),
            pl.BlockSpec((bsz, tb, dk), lambda h, i: (0, i, heads + h)),
            pl.BlockSpec((bsz, tb, dv), lambda h, i: (0, i, heads + h)),
            pl.BlockSpec((bsz, tb, dv), lambda h, i: (0, i, 2 * heads + h)),
            pl.BlockSpec((bsz, tb, dk), lambda h, i: (0, i, h)),
            pl.BlockSpec((1, dv), lambda h, i: (0, h)),
            pl.BlockSpec((1, dv), lambda h, i: (0, h)),
        ],
        out_specs=pl.BlockSpec((bsz, tb, dv), lambda h, i: (0, i, h)),
        out_shape=jax.ShapeDtypeStruct((bsz, s_len, heads * dv), BF16),
        scratch_shapes=[pltpu.VMEM((bsz, dv, dk), F32), pltpu.VMEM((bsz, tb, dv), F32)],
        compiler_params=_cparams(2),
        name="gla_mixer",
    )(proj3, proj3, proj3, proj3, g3, gn_g, gn_b)
    return y.reshape(t_len, heads * dv)


def _layer_norm_rows(z, g_ref, b_ref):
    mu = jnp.mean(z, axis=-1, keepdims=True)
    var = jnp.mean(jnp.square(z - mu), axis=-1, keepdims=True)
    return (z - mu) * lax.rsqrt(var + LN_EPS) * g_ref[...] + b_ref[...]


def _ln_cast_kernel(z_ref, g_ref, b_ref, ob_ref):
    ob_ref[...] = _layer_norm_rows(z_ref[...], g_ref, b_ref).astype(ob_ref.dtype)


def _ln_res_ln_kernel(z_ref, mix_ref, g1_ref, b1_ref, g2_ref, b2_ref, *out_refs):
    h = _layer_norm_rows(z_ref[...], g1_ref, b1_ref)
    y = _layer_norm_rows(DEEPNORM_ALPHA * h + mix_ref[...], g2_ref, b2_ref)
    out_refs[0][...] = y
    for ob_ref in out_refs[1:]:
        n_j = ob_ref.shape[1]
        y3 = y.reshape(n_j, PERM_RES, y.shape[-1])
        ob_ref[...] = pltpu.einshape("jrd->rjd", y3).astype(ob_ref.dtype)


def _ln_specs(d, tm):
    return pl.BlockSpec((tm, d), lambda i: (i, 0)), pl.BlockSpec((1, d), lambda i: (0, 0))


def _ln_cast(z, gain, bias, *, tm=256, name):
    m, d = z.shape
    row_spec, vec_spec = _ln_specs(d, tm)
    return pl.pallas_call(
        _ln_cast_kernel,
        grid=(m // tm,),
        in_specs=[row_spec, vec_spec, vec_spec],
        out_specs=row_spec,
        out_shape=jax.ShapeDtypeStruct((m, d), BF16),
        compiler_params=_cparams(1),
        name=name,
    )(z, gain.reshape(1, d), bias.reshape(1, d))


def _ln_res_ln(z, mix, gain1, bias1, gain2, bias2, *, perm_bf16, tm=256, name):
    m, d = z.shape
    row_spec, vec_spec = _ln_specs(d, tm)
    out_shape = [jax.ShapeDtypeStruct((m, d), F32)]
    out_specs = [row_spec]
    if perm_bf16:
        assert PERM_SPAN % tm == 0 and tm % PERM_RES == 0 and m % PERM_SPAN == 0
        tiles_per_span = PERM_SPAN // tm
        out_shape.append(jax.ShapeDtypeStruct((m // PERM_SPAN, PERM_RES, PERM_RUN, d), BF16))
        out_specs.append(pl.BlockSpec(
            (None, PERM_RES, tm // PERM_RES, d),
            lambda i: (i // tiles_per_span, 0, i % tiles_per_span, 0)))
    vecs = [v.reshape(1, d) for v in (gain1, bias1, gain2, bias2)]
    outs = pl.pallas_call(
        _ln_res_ln_kernel,
        grid=(m // tm,),
        in_specs=[row_spec, row_spec] + [vec_spec] * 4,
        out_specs=out_specs,
        out_shape=out_shape,
        compiler_params=_cparams(1),
        name=name,
    )(z, mix, *vecs)
    return [outs[0]] + [o.reshape(m, d) for o in outs[1:]]


def _dilated_kernel(q1_ref, q4_ref, q16_ref, k1_ref, k4_ref, k16_ref,
                    v1_ref, v4_ref, v16_ref, o_ref, o1_scr, lse1_scr, o4_scr, lse4_scr,
                    otok_scr, *, spans, unroll):
    blk = 128
    scale = DIL_HD ** -0.5
    nt = (((1,), (1,)), ((), ()))
    ridx = lax.broadcasted_iota(jnp.int32, (blk, blk), 0)
    cidx = lax.broadcasted_iota(jnp.int32, (blk, blk), 1)

    def gather(ref, starts, run):
        if len(starts) == 1:
            return ref[pl.ds(starts[0], run), :]
        return jnp.concatenate([ref[pl.ds(s, run), :] for s in starts], axis=0)

    def scatter(ref, starts, run, val):
        for n, s in enumerate(starts):
            ref[pl.ds(s, run), :] = val[n * run:(n + 1) * run, :]

    def branch(q_ref, k_ref, v_ref, n_blocks, starts_fn, run, pos_fn, finish):
        pos_q = pos_fn(ridx)
        pos_k = pos_fn(cidx)
        mask = jnp.concatenate([pos_k >= pos_q, pos_k <= pos_q], axis=1)

        def one_block(n):
            cur, prev, has_prev = starts_fn(n)
            q = gather(q_ref, cur, run).astype(BF16)
            kk = jnp.concatenate([gather(k_ref, prev, run), gather(k_ref, cur, run)], axis=0)
            vv = jnp.concatenate([gather(v_ref, prev, run), gather(v_ref, cur, run)], axis=0)
            s = lax.dot_general(q, kk.astype(BF16), nt, preferred_element_type=F32)
            s = jnp.where(mask, s, -jnp.inf)
            no_prev = jnp.where(has_prev, 0.0, -jnp.inf)
            s = jnp.concatenate([s[:, :blk] + no_prev, s[:, blk:]], axis=1)
            m = jnp.max(s, axis=-1, keepdims=True)
            p = jnp.exp2((s - m) * (scale * LOG2_E))
            l = jnp.sum(p, axis=-1, keepdims=True)
            o = jnp.dot(p.astype(BF16), vv.astype(BF16), preferred_element_type=F32) / l
            lse = jnp.broadcast_to(m * scale + jnp.log(l), (blk, blk))
            finish(n, cur, run, o, lse)

        def body(it, carry):
            for u in range(unroll):
                one_block(it * unroll + u)
            return carry

        assert n_blocks % unroll == 0
        lax.fori_loop(0, n_blocks // unroll, body, 0)

    def al(x, mult):
        return pl.multiple_of(x, mult)

    def divmod_pow2(x, d):
        shift = d.bit_length() - 1
        assert d == 1 << shift
        return x >> shift, x & (d - 1)

    def keep(o_scr, lse_scr):
        def finish(n, cur, run, o, lse):
            scatter(o_scr, cur, run, o)
            scatter(lse_scr, cur, run, lse)
        return finish

    def starts_d1(n):
        def runs(b):
            span, sub = divmod_pow2(b, 16)
            return [al(span * PERM_SPAN + r * PERM_RUN + sub * 8, 8) for r in range(PERM_RES)]
        return runs(n), runs(jnp.maximum(n - 1, 0)), n > 0

    branch(q1_ref, k1_ref, v1_ref, spans * 16, starts_d1, 8,
           lambda i: 16 * (i & 7) + (i >> 3), keep(o1_scr, lse1_scr))

    def starts_d4(n):
        res, c = divmod_pow2(n, spans * 4)
        def runs(cc):
            span, sub = divmod_pow2(cc, 4)
            return [al(span * PERM_SPAN + (res + 4 * u) * PERM_RUN + sub * 32, 32) for u in range(4)]
        return runs(c), runs(jnp.maximum(c - 1, 0)), c > 0

    branch(q4_ref, k4_ref, v4_ref, spans * 16, starts_d4, 32,
           lambda i: 4 * (i & 31) + (i >> 5), keep(o4_scr, lse4_scr))

    def starts_d16(n):
        res, span = divmod_pow2(n, spans)
        def runs(sp):
            return [al(sp * PERM_SPAN + res * PERM_RUN, PERM_RUN)]
        return runs(span), runs(jnp.maximum(span - 1, 0)), span > 0

    def combine(n, cur, run, o16, lse16):
        lse1 = gather(lse1_scr, cur, run)
        lse4 = gather(lse4_scr, cur, run)
        top = jnp.maximum(jnp.maximum(lse1, lse4), lse16)
        e1 = jnp.exp(lse1 - top)
        e4 = jnp.exp(lse4 - top)
        e16 = jnp.exp(lse16 - top)
        den = e1 + e4 + e16
        o = (e1 / den) * gather(o1_scr, cur, run) + (e4 / den) * gather(o4_scr, cur, run)
        o = o + (e16 / den) * o16
        res, span = divmod_pow2(n, spans)
        otok_scr[pl.ds(span * PERM_SPAN + res, PERM_RUN, stride=PERM_RES), :] = o

    branch(q16_ref, k16_ref, v16_ref, spans * 16, starts_d16, 128, lambda i: i, combine)
    o_ref[...] = otok_scr[...].astype(o_ref.dtype)


def _dilated_attention(q, kv, *, bsz, s_len, heads, unroll=32):
    t_len = q.shape[0]
    hd = DIL_HD
    assert s_len % PERM_SPAN == 0
    spans = s_len // PERM_SPAN
    assert q.shape[1] == 3 * heads * hd and kv.shape[1] == 6 * heads * hd

    def qspec(g):
        return pl.BlockSpec((s_len, hd), lambda b, h: (b, g * heads + h))

    def kvspec(g, which):
        return pl.BlockSpec((s_len, hd), lambda b, h: (b, (2 * g + which) * heads + h))

    return pl.pallas_call(
        functools.partial(_dilated_kernel, spans=spans, unroll=unroll),
        grid=(bsz, heads),
        in_specs=[qspec(0), qspec(1), qspec(2),
                  kvspec(0, 0), kvspec(1, 0), kvspec(2, 0),
                  kvspec(0, 1), kvspec(1, 1), kvspec(2, 1)],
        out_specs=pl.BlockSpec((s_len, hd), lambda b, h: (b, h)),
        out_shape=jax.ShapeDtypeStruct((t_len, heads * hd), BF16),
        scratch_shapes=[pltpu.VMEM((s_len, hd), F32)] * 5,
        compiler_params=_cparams(2),
        name="dilated_attention",
    )(q, q, q, kv, kv, kv, kv, kv, kv)


def _channel_mix(z, w1, w2, ln_g, ln_b, layer, *, perm_bf16):
    hb = _ln_cast(z, ln_g[layer, 0], ln_b[layer, 0], name=f"ln_mix_{layer}")
    hmid = _matmul(hb, w1, layer, out_dtype=BF16, act="relu2", name=f"mlp_up_{layer}")
    mix = _matmul_ktiled(hmid, w2, layer, name=f"mlp_down_{layer}")
    return _ln_res_ln(z, mix, ln_g[layer, 0], ln_b[layer, 0], ln_g[layer, 1], ln_b[layer, 1],
                      perm_bf16=perm_bf16, name=f"ln_mlp_{layer}")


def kernel(x, a_w_in, a_w_g1, a_w_g2, a_b_g, a_gn_g, a_gn_b, a_w_out, b_w_q, kv_w, b_w_out,
           mlp_w1, mlp_w2, ln_g, ln_b):
    bsz, s_len, d = x.shape
    t_len = bsz * s_len
    h = x.reshape(t_len, d)

    g, hb = _gate(h, a_w_g1[0], a_w_g2[0], a_b_g[0].reshape(1, -1))
    proj = _matmul(hb, a_w_in, 0, out_dtype=F32, name="gla_in_proj")
    y = _gla(proj, g, a_gn_g[0].reshape(1, -1), a_gn_b[0].reshape(1, -1),
             bsz=bsz, s_len=s_len, heads=GLA_HEADS)
    z = _matmul(y, a_w_out, 0, out_dtype=F32, residual=h, res_scale=DEEPNORM_ALPHA,
                tm=1024, x_buffers=2, name="gla_out_proj")
    h, hb_perm = _channel_mix(z, mlp_w1, mlp_w2, ln_g, ln_b, 0, perm_bf16=True)

    q =_matmul(hb_perm, b_w_q, 0, out_dtype=F32, name="dil_q_proj")
    kv = _matmul(hb_perm, kv_w, 0, out_dtype=F32, name="dil_kv_proj")
    o = _dilated_attention(q, kv, bsz=bsz, s_len=s_len, heads=DIL_HEADS)
    z = _matmul(o, b_w_out, 0, out_dtype=F32, residual=h, res_scale=DEEPNORM_ALPHA,
                tm=1024, x_buffers=2, name="dil_out_proj")
    (h,) = _channel_mix(z, mlp_w1, mlp_w2, ln_g, ln_b, 1, perm_bf16=False)
    return h.reshape(bsz, s_len, d)
```
